```python
import math
import jax, jax.numpy as jnp
from jax import lax
import numpy as np

D_MODEL = 1024
BATCH = 2
SEQ = 8192
DEPTH = 4

GRID_W = 64
CHUNK = 128
EPS = 1e-6
MIX_WIDTH = D_MODEL
RET_HEADS = 8
RET_DH = (MIX_WIDTH // 2) // RET_HEADS
RET_WIDTH = RET_HEADS * RET_DH
ROPE_BASE = 10000.0
NA_HEADS = 8
NA_DH = (MIX_WIDTH // 2) // NA_HEADS
NA_WIDTH = NA_HEADS * NA_DH
NA_WIN_R = 8
NA_WIN_C = 16
NA_QBLK = 16
AB_IN_DIM = 4 * RET_WIDTH + 3 * NA_WIDTH
SSD_INNER = 2 * D_MODEL
SSD_HEADDIM = 64
SSD_HEADS = SSD_INNER // SSD_HEADDIM
SSD_GROUPS = 4
SSD_HPG = SSD_HEADS // SSD_GROUPS
SSD_STATE = 128
SSD_CONV = 5
SSD_XBC = SSD_INNER + 2 * SSD_GROUPS * SSD_STATE
SSD_IN_DIM = SSD_INNER + SSD_XBC + 2 * SSD_HEADS
FFN_DIM = 2816
FFN_CONV = 3
N_EVEN = (DEPTH + 1) // 2
N_ODD = DEPTH // 2

kernel_name = 'hybrid_retention_natten_ssd_encoder'


def rms_norm(x, g):
    xf = x.astype(jnp.float32)
    y = xf * lax.rsqrt(jnp.mean(xf * xf, axis=-1, keepdims=True) + EPS)
    return (y * g.astype(jnp.float32)).astype(x.dtype)


def depthwise_conv_centered(x, w, b):
    width, ch = w.shape
    pad = width // 2
    y = lax.conv_general_dilated(x, w[:, None, :].astype(x.dtype), window_strides=(1,),
                                 padding=[(pad, pad)], dimension_numbers=('NWC', 'WIO', 'NWC'),
                                 feature_group_count=ch)
    return y + b.astype(x.dtype)


def rotary(x, pos):
    half = x.shape[-1] // 2
    inv = 1.0 / (ROPE_BASE ** (jnp.arange(half, dtype=jnp.float32) / half))
    ang = pos.astype(jnp.float32)[:, None] * inv[None, :]
    cos = jnp.cos(ang)[None, :, None, :]
    sin = jnp.sin(ang)[None, :, None, :]
    xf = x.astype(jnp.float32)
    x1, x2 = xf[..., :half], xf[..., half:]
    return jnp.concatenate([x1 * cos - x2 * sin, x1 * sin + x2 * cos], axis=-1).astype(x.dtype)


def chunked_scan(q, k, v, log_a, include_diag):
    f32 = jnp.float32
    bsz, s, g, n = q.shape
    hg, p = v.shape[3], v.shape[4]
    n_chunks = s // CHUNK

    def to_chunks(t):
        return jnp.moveaxis(t.astype(f32).reshape(bsz, n_chunks, CHUNK, *t.shape[2:]), 1, 0)

    qc, kc, vc, ac = to_chunks(q), to_chunks(k), to_chunks(v), to_chunks(log_a)
    idx = jnp.arange(CHUNK)
    mask = (idx[:, None] >= idx[None, :]) if include_diag else (idx[:, None] > idx[None, :])

    def step(h, inp):
        qq, kk, vv, aa = inp
        cs = jnp.cumsum(aa, axis=1)
        seg = cs[:, :, None] - cs[:, None, :]
        decay = jnp.exp(jnp.where(mask[None, :, :, None, None], seg, -jnp.inf))
        qk = jnp.einsum('bjgn,blgn->bjlg', qq, kk)
        y_intra = jnp.einsum('bjlgh,blghp->bjghp', qk[..., None] * decay, vv)
        y_inter = jnp.einsum('bjgn,bghnp->bjghp', qq, h) * jnp.exp(cs)[..., None]
        tail = jnp.exp(cs[:, -1:] - cs)
        h_new = h * jnp.exp(cs[:, -1])[..., None, None] + jnp.einsum('blgn,blghp->bghnp', kk, vv * tail[..., None])
        return h_new, y_intra + y_inter

    h0 = jnp.zeros((bsz, g, hg, n, p), f32)
    _, ys = lax.scan(step, h0, (qc, kc, vc, ac))
    return jnp.moveaxis(ys, 0, 1).reshape(bsz, s, g, hg, p)


def bidir_scan(q, k, v_f, v_b, a_f, a_b):
    y_f = chunked_scan(q, k, v_f, a_f, True)
    flip = lambda t: jnp.flip(t, axis=1)
    y_b = flip(chunked_scan(flip(q), flip(k), flip(v_b), flip(a_b), False))
    return y_f + y_b


def neighborhood_attention(q, k, v, rpb):
    f32 = jnp.float32
    bsz, s, h, dh = q.shape
    rows = s // GRID_W
    win_r = min(NA_WIN_R, rows)
    n_cb = GRID_W // NA_QBLK
    span = NA_QBLK + NA_WIN_C
    qb_all = (q.astype(f32) * dh ** -0.5).reshape(bsz, rows, n_cb, NA_QBLK, h, dh)
    qb_all = jnp.moveaxis(qb_all, 2, 0)
    k = k.reshape(bsz, rows, GRID_W, h, dh)
    v = v.reshape(bsz, rows, GRID_W, h, dh)
    r = jnp.arange(rows)
    key_rows = jnp.clip(r - win_r // 2, 0, rows - win_r)[:, None] + jnp.arange(win_r)[None, :]
    dr = key_rows - r[:, None] + (NA_WIN_R - 1)
    c0s = jnp.arange(n_cb) * NA_QBLK

    def block(args):
        qb, c0 = args
        qcols = c0 + jnp.arange(NA_QBLK)
        kcols = jnp.clip(c0 - NA_WIN_C // 2, 0, GRID_W - span) + jnp.arange(span)
        cstart = jnp.clip(qcols - NA_WIN_C // 2, 0, GRID_W - NA_WIN_C)
        valid = (kcols[None, :] >= cstart[:, None]) & (kcols[None, :] < cstart[:, None] + NA_WIN_C)
        dc = jnp.clip(kcols[None, :] - qcols[:, None], -(NA_WIN_C - 1), NA_WIN_C - 1) + (NA_WIN_C - 1)
        kb = k[:, key_rows[:, :, None], kcols[None, None, :]].astype(f32)
        vb = v[:, key_rows[:, :, None], kcols[None, None, :]].astype(f32)
        bias = rpb[:, dr[:, None, :, None], dc[None, :, None, :]].astype(f32)
        sc = jnp.einsum('brqhd,brwchd->bhrqwc', qb, kb) + bias[None]
        sc = jnp.where(valid[:, None, :], sc, -jnp.inf)
        pr = jax.nn.softmax(sc.reshape(*sc.shape[:4], -1), axis=-1).reshape(sc.shape)
        return jnp.einsum('bhrqwc,brwchd->brqhd', pr, vb)

    out = lax.map(block, (qb_all, c0s))
    return jnp.moveaxis(out, 0, 2).reshape(bsz, s, h * dh)


def retention_na_mixer(hn, w_in, ret_decay_logit, ret_gn_g, na_rpb, w_out):
    f32 = jnp.float32
    bsz, s, _ = hn.shape
    proj = hn @ w_in
    R, N = RET_WIDTH, NA_WIDTH
    rq, rk, rv, rg, nq, nk, nv = jnp.split(proj, [R, 2 * R, 3 * R, 4 * R, 4 * R + N, 4 * R + 2 * N], axis=-1)
    pos = jnp.arange(s)
    rshape = (bsz, s, RET_HEADS, RET_DH)
    rq = rotary(rq.reshape(rshape), pos)
    rk = rotary(rk.reshape(rshape), pos) * (RET_DH ** -0.5)
    rv = rv.reshape(bsz, s, RET_HEADS, 1, RET_DH)
    log_gamma = -jax.nn.softplus(-ret_decay_logit.astype(f32))
    a_f = jnp.broadcast_to(log_gamma[0][None, None, :, None], (bsz, s, RET_HEADS, 1))
    a_b = jnp.broadcast_to(log_gamma[1][None, None, :, None], (bsz, s, RET_HEADS, 1))
    y = bidir_scan(rq, rk, rv, rv, a_f, a_b).reshape(rshape)
    mu = jnp.mean(y, axis=-1, keepdims=True)
    var = jnp.mean(jnp.square(y - mu), axis=-1, keepdims=True)
    y = ((y - mu) * lax.rsqrt(var + EPS)).reshape(bsz, s, RET_WIDTH) * ret_gn_g.astype(f32)
    ret_out = (jax.nn.silu(rg.astype(f32)) * y).astype(hn.dtype)
    nshape = (bsz, s, NA_HEADS, NA_DH)
    na_out = neighborhood_attention(nq.reshape(nshape), nk.reshape(nshape), nv.reshape(nshape), na_rpb).astype(hn.dtype)
    return jnp.concatenate([ret_out, na_out], axis=-1) @ w_out


def ssd_mixer(hn, w_in, conv_w, conv_b, dt_bias, a_log, d_skip, norm_g, w_out):
    f32 = jnp.float32
    bsz, s, _ = hn.shape
    proj = hn @ w_in
    z, xbc, dt_raw = jnp.split(proj, [SSD_INNER, SSD_INNER + SSD_XBC], axis=-1)
    xbc = jax.nn.silu(depthwise_conv_centered(xbc, conv_w, conv_b))
    xs, bm, cm = jnp.split(xbc, [SSD_INNER, SSD_INNER + SSD_GROUPS * SSD_STATE], axis=-1)
    dt = jax.nn.softplus(dt_raw.astype(f32).reshape(bsz, s, 2, SSD_HEADS) + dt_bias.astype(f32))
    A = -jnp.exp(a_log.astype(f32))
    log_a = dt * A
    grp = (bsz, s, SSD_GROUPS, SSD_HPG)
    xh = xs.astype(f32).reshape(bsz, s, SSD_GROUPS, SSD_HPG, SSD_HEADDIM)
    v_f = xh * dt[:, :, 0].reshape(grp)[..., None]
    v_b = xh * dt[:, :, 1].reshape(grp)[..., None]
    qc = cm.reshape(bsz, s, SSD_GROUPS, SSD_STATE)
    kb = bm.reshape(bsz, s, SSD_GROUPS, SSD_STATE)
    y = bidir_scan(qc, kb, v_f, v_b, log_a[:, :, 0].reshape(grp), log_a[:, :, 1].reshape(grp))
    y = y + xh * d_skip.astype(f32).reshape(SSD_GROUPS, SSD_HPG)[..., None]
    y = y.reshape(bsz, s, SSD_INNER) * jax.nn.silu(z.astype(f32))
    yg = y.reshape(bsz, s, SSD_GROUPS, SSD_INNER // SSD_GROUPS)
    yg = yg * lax.rsqrt(jnp.mean(yg * yg, axis=-1, keepdims=True) + EPS)
    y = yg.reshape(bsz, s, SSD_INNER) * norm_g.astype(f32)
    return y.astype(hn.dtype) @ w_out


def conv_geglu_ffn(hn, w_up, conv_w, conv_b, w_down):
    u = depthwise_conv_centered(hn @ w_up, conv_w, conv_b)
    gate, val = jnp.split(u, 2, axis=-1)
    return (jax.nn.gelu(gate, approximate=True) * val) @ w_down


def setup_inputs(seed: int = 0) -> dict:
    key = jax.random.key(seed)
    ks = jax.random.split(key, 24)
    f32 = jnp.float32

    def nrm(k, shape, scale):
        return jax.random.normal(k, shape, f32) * scale

    def gain(k, shape):
        return 1.0 + 0.05 * jax.random.normal(k, shape, f32)

    x = nrm(ks[0], (BATCH, SEQ, D_MODEL), 1.0)
    norm_mix_pre = gain(ks[1], (DEPTH, D_MODEL))
    norm_mix_post = gain(ks[2], (DEPTH, D_MODEL))
    norm_ffn_pre = gain(ks[3], (DEPTH, D_MODEL))
    norm_ffn_post = gain(ks[4], (DEPTH, D_MODEL))
    ab_w_in = nrm(ks[5], (N_EVEN, D_MODEL, AB_IN_DIM), D_MODEL ** -0.5)
    gamma0 = 1.0 - 2.0 ** (-5.0 - jnp.arange(RET_HEADS, dtype=f32))
    ab_ret_decay_logit = (jnp.log(gamma0) - jnp.log1p(-gamma0))[None, None, :] + nrm(ks[6], (N_EVEN, 2, RET_HEADS), 0.05)
    ab_ret_gn_g = gain(ks[7], (N_EVEN, RET_WIDTH))
    ab_na_rpb = nrm(ks[8], (N_EVEN, NA_HEADS, 2 * NA_WIN_R - 1, 2 * NA_WIN_C - 1), 0.1)
    ab_w_out = nrm(ks[9], (N_EVEN, RET_WIDTH + NA_WIDTH, D_MODEL), (RET_WIDTH + NA_WIDTH) ** -0.5)
    c_w_in = nrm(ks[10], (N_ODD, D_MODEL, SSD_IN_DIM), D_MODEL ** -0.5)
    c_conv_w = nrm(ks[11], (N_ODD, SSD_CONV, SSD_XBC), SSD_CONV ** -0.5)
    c_conv_b = nrm(ks[12], (N_ODD, SSD_XBC), 0.02)
    dt0 = jnp.exp(jax.random.uniform(ks[13], (N_ODD, 2, SSD_HEADS), f32, math.log(1e-3), math.log(1e-1)))
    c_dt_bias = dt0 + jnp.log(-jnp.expm1(-dt0))
    c_a_log = jnp.log(jax.random.uniform(ks[14], (N_ODD, 2, SSD_HEADS), f32, 1.0, 16.0))
    c_d_skip = 1.0 + 0.1 * jax.random.normal(ks[15], (N_ODD, SSD_HEADS), f32)
    c_norm_g = gain(ks[16], (N_ODD, SSD_INNER))
    c_w_out = nrm(ks[17], (N_ODD, SSD_INNER, D_MODEL), SSD_INNER ** -0.5)
    ffn_w_up = nrm(ks[18], (DEPTH, D_MODEL, 2 * FFN_DIM), D_MODEL ** -0.5)
    ffn_conv_w = nrm(ks[19], (DEPTH, FFN_CONV, 2 * FFN_DIM), FFN_CONV ** -0.5)
    ffn_conv_b = nrm(ks[20], (DEPTH, 2 * FFN_DIM), 0.02)
    ffn_w_down = nrm(ks[21], (DEPTH, FFN_DIM, D_MODEL), FFN_DIM ** -0.5)
    return {'x': x, 'norm_mix_pre': norm_mix_pre, 'norm_mix_post': norm_mix_post,
            'norm_ffn_pre': norm_ffn_pre, 'norm_ffn_post': norm_ffn_post,
            'ab_w_in': ab_w_in, 'ab_ret_decay_logit': ab_ret_decay_logit, 'ab_ret_gn_g': ab_ret_gn_g,
            'ab_na_rpb': ab_na_rpb, 'ab_w_out': ab_w_out,
            'c_w_in': c_w_in, 'c_conv_w': c_conv_w, 'c_conv_b': c_conv_b, 'c_dt_bias': c_dt_bias,
            'c_a_log': c_a_log, 'c_d_skip': c_d_skip, 'c_norm_g': c_norm_g, 'c_w_out': c_w_out,
            'ffn_w_up': ffn_w_up, 'ffn_conv_w': ffn_conv_w, 'ffn_conv_b': ffn_conv_b, 'ffn_w_down': ffn_w_down}


def reference(x, norm_mix_pre, norm_mix_post, norm_ffn_pre, norm_ffn_post,
              ab_w_in, ab_ret_decay_logit, ab_ret_gn_g, ab_na_rpb, ab_w_out,
              c_w_in, c_conv_w, c_conv_b, c_dt_bias, c_a_log, c_d_skip, c_norm_g, c_w_out,
              ffn_w_up, ffn_conv_w, ffn_conv_b, ffn_w_down):
    for layer in range(DEPTH):
        i = layer // 2
        hn = rms_norm(x, norm_mix_pre[layer])
        if layer % 2 == 0:
            m = retention_na_mixer(hn, ab_w_in[i], ab_ret_decay_logit[i], ab_ret_gn_g[i], ab_na_rpb[i], ab_w_out[i])
        else:
            m = ssd_mixer(hn, c_w_in[i], c_conv_w[i], c_conv_b[i], c_dt_bias[i], c_a_log[i],
                          c_d_skip[i], c_norm_g[i], c_w_out[i])
        x = x + rms_norm(m, norm_mix_post[layer])
        f = conv_geglu_ffn(rms_norm(x, norm_ffn_pre[layer]), ffn_w_up[layer], ffn_conv_w[layer],
                           ffn_conv_b[layer], ffn_w_down[layer])
        x = x + rms_norm(f, norm_ffn_post[layer])
    return x
```

```python
import functools
import math

import jax
import jax.numpy as jnp
from jax import lax
from jax.experimental import pallas as pl
from jax.experimental.pallas import tpu as pltpu

F32 = jnp.float32
BF16 = jnp.bfloat16

EPS = 1e-6
GRID_W = 64
ROPE_BASE = 10000.0
RET_HEADS = 8
NA_HEADS = 8
NA_WIN_R = 8
NA_WIN_C = 16
SSD_HEADDIM = 64
SSD_GROUPS = 4
SSD_STATE = 128
HEAD_DIM = 64
LANES = 128
HALO = 8
SCAN_CHUNK = 128
NA_ROWS_PER_STEP = 16
NEG_BIG = -1e30
VMEM_LIMIT = 56 * 1024 * 1024


def _params(*sem):
    return pltpu.CompilerParams(dimension_semantics=sem, vmem_limit_bytes=VMEM_LIMIT)


def _resident(shape):
    nd = len(shape)
    return pl.BlockSpec(shape, lambda *_: (0,) * nd, pipeline_mode=pl.Buffered(1))


def _rms_rows(x, g):
    return x * lax.rsqrt(jnp.mean(x * x, axis=-1, keepdims=True) + EPS) * g


def _split3_dot(tri, a):
    hi = a.astype(BF16)
    r1 = a - hi.astype(F32)
    mid = r1.astype(BF16)
    lo = (r1 - mid.astype(F32)).astype(BF16)
    return (jnp.dot(tri, hi, preferred_element_type=F32)
            + jnp.dot(tri, mid, preferred_element_type=F32)
            + jnp.dot(tri, lo, preferred_element_type=F32))


def _split2_dot(a, m):
    hi = a.astype(BF16)
    lo = (a - hi.astype(F32)).astype(BF16)
    return jnp.dot(hi, m, preferred_element_type=F32) + jnp.dot(lo, m, preferred_element_type=F32)


def _norm_proj_kernel(x_ref, g_ref, w_ref, o_ref, *, ncol):
    hn = _rms_rows(x_ref[...], g_ref[...]).astype(BF16)
    for c in range(0, w_ref.shape[1], ncol):
        o_ref[:, c:c + ncol] = jnp.dot(hn, w_ref[:, c:c + ncol],
                                       preferred_element_type=F32).astype(o_ref.dtype)


def _norm_proj(x, g, w, tm=512, ncol=512):
    t, d = x.shape
    n = w.shape[1]
    return pl.pallas_call(
        functools.partial(_norm_proj_kernel, ncol=ncol),
        grid=(t // tm,),
        in_specs=[pl.BlockSpec((tm, d), lambda i: (i, 0)), _resident((1, d)), _resident((d, n))],
        out_specs=pl.BlockSpec((tm, n), lambda i: (i, 0)),
        out_shape=jax.ShapeDtypeStruct((t, n), BF16),
        compiler_params=_params("parallel"),
        name="norm_proj",
    )(x, g, w)


def _out_proj_kernel(*refs, n_act):
    acts = refs[:n_act]
    w_ref, g_ref, x_ref, o_ref = refs[n_act:]
    k0 = 0
    m = None
    for a in acts:
        k = a.shape[1]
        part = jnp.dot(a[...], w_ref[k0:k0 + k, :], preferred_element_type=F32)
        m = part if m is None else m + part
        k0 += k
    o_ref[...] = x_ref[...] + _rms_rows(m, g_ref[...])


def _out_proj(acts, w, g, x, tm=512):
    t, d = x.shape
    in_specs = [pl.BlockSpec((tm, a.shape[1]), lambda i: (i, 0)) for a in acts]
    in_specs += [_resident(w.shape), _resident((1, d)), pl.BlockSpec((tm, d), lambda i: (i, 0))]
    return pl.pallas_call(
        functools.partial(_out_proj_kernel, n_act=len(acts)),
        grid=(t // tm,),
        in_specs=in_specs,
        out_specs=pl.BlockSpec((tm, d), lambda i: (i, 0)),
        out_shape=jax.ShapeDtypeStruct((t, d), F32),
        compiler_params=_params("parallel"),
        name="out_proj",
    )(*acts, w, g, x)


def _halo_specs(tm, d, t):
    nb = tm // HALO
    last = t // HALO - 1
    return [
        pl.BlockSpec((HALO, d), lambda i: (jnp.maximum(i * nb - 1, 0), 0)),
        pl.BlockSpec((tm, d), lambda i: (i, 0)),
        pl.BlockSpec((HALO, d), lambda i: (jnp.minimum((i + 1) * nb, last), 0)),
    ]


def _normed_tile_with_halo(xp_ref, x_ref, xn_ref, g_ref, tiles_per_seq):
    i = pl.program_id(0)
    pos = i % tiles_per_seq
    has_prev = (pos != 0).astype(F32)
    has_next = (pos != tiles_per_seq - 1).astype(F32)
    g = g_ref[...]
    top = _rms_rows(xp_ref[...], g) * has_prev
    mid = _rms_rows(x_ref[...], g)
    bot = _rms_rows(xn_ref[...], g) * has_next
    return jnp.concatenate([top, mid, bot], axis=0).astype(BF16)


def _depthwise_conv_rows(u, w_ref, b_ref, cols, tm):
    n = u.shape[0]
    width = w_ref.shape[0]
    pad = width // 2
    acc = None
    for k in range(width):
        shift = pad - k
        us = u if shift == 0 else pltpu.roll(u, shift % n, 0)
        term = us[HALO:HALO + tm] * w_ref[k:k + 1, cols]
        acc = term if acc is None else acc + term
    return acc + b_ref[:, cols]


def _ffn_kernel(xp_ref, x_ref, xn_ref, gpre_ref, wup_ref, cw_ref, cb_ref, wdn_ref, gpost_ref,
                o_ref, *, tiles_per_seq, ncol):
    tm = x_ref.shape[0]
    f = wdn_ref.shape[0]
    hn = _normed_tile_with_halo(xp_ref, x_ref, xn_ref, gpre_ref, tiles_per_seq)
    acc = None
    for c in range(0, f, ncol):
        gcols = slice(c, c + ncol)
        vcols = slice(f + c, f + c + ncol)
        ug = jnp.dot(hn, wup_ref[:, gcols], preferred_element_type=F32)
        uv = jnp.dot(hn, wup_ref[:, vcols], preferred_element_type=F32)
        gate = _depthwise_conv_rows(ug, cw_ref, cb_ref, gcols, tm)
        val = _depthwise_conv_rows(uv, cw_ref, cb_ref, vcols, tm)
        h = (jax.nn.gelu(gate, approximate=True) * val).astype(BF16)
        part = jnp.dot(h, wdn_ref[gcols, :], preferred_element_type=F32)
        acc = part if acc is None else acc + part
    o_ref[...] = x_ref[...] + _rms_rows(acc, gpost_ref[...])


def _ffn(x, gpre, wup, cw, cb, wdn, gpost, seq, tm=512, ncol=256):
    t, d = x.shape
    f = wdn.shape[0]
    return pl.pallas_call(
        functools.partial(_ffn_kernel, tiles_per_seq=seq // tm, ncol=ncol),
        grid=(t // tm,),
        in_specs=_halo_specs(tm, d, t) + [
            _resident((1, d)), _resident(wup.shape), _resident(cw.shape), _resident((1, 2 * f)),
            _resident(wdn.shape), _resident((1, d))],
        out_specs=pl.BlockSpec((tm, d), lambda i: (i, 0)),
        out_shape=jax.ShapeDtypeStruct((t, d), F32),
        compiler_params=_params("parallel"),
        name="ffn",
    )(x, x, x, gpre, wup, cw, cb, wdn, gpost)


def _ssd_inproj_kernel(xp_ref, x_ref, xn_ref, g_ref, wz_ref, wx_ref, wdt_ref, cw_ref, cb_ref,
                       z_ref, xbc_ref, dt_ref, *, tiles_per_seq, ncol):
    tm = x_ref.shape[0]
    hn = _normed_tile_with_halo(xp_ref, x_ref, xn_ref, g_ref, tiles_per_seq)
    hm = hn[HALO:HALO + tm]
    for c in range(0, wz_ref.shape[1], ncol):
        z_ref[:, c:c + ncol] = jnp.dot(hm, wz_ref[:, c:c + ncol],
                                       preferred_element_type=F32).astype(z_ref.dtype)
    dt_ref[...] = jnp.dot(hm, wdt_ref[...], preferred_element_type=F32)
    for c in range(0, wx_ref.shape[1], ncol):
        cols = slice(c, c + ncol)
        u = jnp.dot(hn, wx_ref[:, cols], preferred_element_type=F32)
        v = _depthwise_conv_rows(u, cw_ref, cb_ref, cols, tm)
        xbc_ref[:, cols] = (v * jax.nn.sigmoid(v)).astype(xbc_ref.dtype)


def _ssd_inproj(x, g, wz, wx, wdt, cw, cb, seq, tm=512, ncol=512):
    t, d = x.shape
    nz, nx, ndt = wz.shape[1], wx.shape[1], wdt.shape[1]
    row = lambda n: pl.BlockSpec((tm, n), lambda i: (i, 0))
    return pl.pallas_call(
        functools.partial(_ssd_inproj_kernel, tiles_per_seq=seq // tm, ncol=ncol),
        grid=(t // tm,),
        in_specs=_halo_specs(tm, d, t) + [
            _resident((1, d)), _resident(wz.shape), _resident(wx.shape), _resident(wdt.shape),
            _resident(cw.shape), _resident((1, nx))],
        out_specs=[row(nz), row(nx), row(ndt)],
        out_shape=[jax.ShapeDtypeStruct((t, nz), BF16), jax.ShapeDtypeStruct((t, nx), BF16),
                   jax.ShapeDtypeStruct((t, ndt), F32)],
        compiler_params=_params("parallel"),
        name="ssd_inproj",
    )(x, x, x, g, wz, wx, wdt, cw, cb)


def _chunk_of(p, c, nc):
    return p * c + (1 - p) * (nc - 1 - c)


def _first_half_mask(shape):
    return lax.broadcasted_iota(jnp.int32, shape, len(shape) - 1) < HEAD_DIM


def _ret_kernel(q_ref, k_ref, v_ref, g_ref, cos_ref, sin_ref, lgf_ref, lgb_ref, gn_ref, gm_ref,
                o_ref, hf, hb, hbs, dm, *, nc):
    p = pl.program_id(1)
    c = pl.program_id(2)
    L = k_ref.shape[0]
    npair = k_ref.shape[1] // LANES
    lgf = lgf_ref[...]
    lgb = lgb_ref[...]
    first = _first_half_mask((1, LANES))
    swap_up = (lax.broadcasted_iota(jnp.int32, (1, LANES), 1) % HEAD_DIM) < HEAD_DIM // 2
    ri = lax.broadcasted_iota(jnp.int32, (LANES, LANES), 0) < HEAD_DIM
    ci = lax.broadcasted_iota(jnp.int32, (LANES, LANES), 1) < HEAD_DIM
    same_head = ri == ci
    jj = lax.broadcasted_iota(jnp.int32, (L, 1), 0).astype(F32)
    cos_f = cos_ref[...]
    sin_s = sin_ref[...]
    scale = HEAD_DIM ** -0.5

    def rot(x):
        sw = jnp.where(swap_up, pltpu.roll(x, LANES - HEAD_DIM // 2, 1),
                       pltpu.roll(x, HEAD_DIM // 2, 1))
        return x * cos_f + sw * sin_s

    @pl.when((p == 0) & (c == 0))
    def _init():
        hf[...] = jnp.zeros_like(hf)
        hb[...] = jnp.zeros_like(hb)
        d = (lax.broadcasted_iota(jnp.int32, (L, L), 0)
             - lax.broadcasted_iota(jnp.int32, (L, L), 1)).astype(F32)
        for h in range(2 * npair):
            lf = lgf[:, HEAD_DIM * h:HEAD_DIM * h + 1]
            lb = lgb[:, HEAD_DIM * h:HEAD_DIM * h + 1]
            dm[h] = jnp.exp(jnp.where(d >= 0, d * lf, -d * lb))

    @pl.when(p == 0)
    def _backward_states():
        ch = nc - 1 - c
        k = k_ref[...].astype(F32)
        tail_b = jnp.exp(jj * lgb)
        decay_b = jnp.exp(L * lgb)
        for pr in range(npair):
            sl = slice(LANES * pr, LANES * (pr + 1))
            kp = rot(k[:, sl]) * scale
            kt = (kp * tail_b[:, sl]).T.astype(BF16)
            hbs[ch, pr] = hb[pr]
            upd = jnp.dot(kt, v_ref[:, sl], preferred_element_type=F32)
            hb[pr] = jnp.where(same_head, hb[pr] * decay_b[:, sl] + upd, 0.0)

    @pl.when(p == 1)
    def _forward():
        q = q_ref[...].astype(F32)
        k = k_ref[...].astype(F32)
        dec_f = jnp.exp((jj + 1.0) * lgf)
        dec_b = jnp.exp((L - jj) * lgb)
        tail_f = jnp.exp((L - 1.0 - jj) * lgf)
        decay_f = jnp.exp(L * lgf)
        ys = []
        for pr in range(npair):
            sl = slice(LANES * pr, LANES * (pr + 1))
            qp = rot(q[:, sl])
            kp = rot(k[:, sl]) * scale
            kb = kp.astype(BF16)
            vp = v_ref[:, sl]
            res = []
            for hh in range(2):
                keep = first if hh == 0 else jnp.logical_not(first)
                qm = jnp.where(keep, qp, 0.0).astype(BF16)
                s = lax.dot_general(qm, kb, (((1,), (1,)), ((), ())), preferred_element_type=F32)
                a = (s * dm[2 * pr + hh]).astype(BF16)
                res.append(jnp.dot(a, vp, preferred_element_type=F32))
            y = jnp.where(first, res[0], res[1])
            y = y + jnp.dot((qp * dec_f[:, sl]).astype(BF16), hf[pr].astype(BF16),
                            preferred_element_type=F32)
            y = y + jnp.dot((qp * dec_b[:, sl]).astype(BF16), hbs[c, pr].astype(BF16),
                            preferred_element_type=F32)
            kt = (kp * tail_f[:, sl]).T.astype(BF16)
            upd = jnp.dot(kt, vp, preferred_element_type=F32)
            hf[pr] = jnp.where(same_head, hf[pr] * decay_f[:, sl] + upd, 0.0)
            ys.append(y)
        y = jnp.concatenate(ys, axis=1)
        gm = gm_ref[...]
        mu = _split2_dot(y, gm)
        d = y - mu
        var = _split2_dot(d * d, gm)
        yn = d * lax.rsqrt(var + EPS) * gn_ref[...]
        g = g_ref[...].astype(F32)
        o_ref[...] = (g * jax.nn.sigmoid(g) * yn).astype(o_ref.dtype)


def _retention(proj, cos_t, sin_t, lgf, lgb, gn, gmat, batch, seq, width):
    L = SCAN_CHUNK
    nc = seq // L
    nw = width // LANES
    t = proj.shape[0]

    def kv_map(col):
        return lambda b, p, c: (b * nc + _chunk_of(p, c, nc), col)

    def fwd_map(col):
        return lambda b, p, c: (b * nc + p * c, col)

    tab_map = lambda b, p, c: (_chunk_of(p, c, nc), 0)
    return pl.pallas_call(
        functools.partial(_ret_kernel, nc=nc),
        grid=(batch, 2, nc),
        in_specs=[pl.BlockSpec((L, width), fwd_map(0)), pl.BlockSpec((L, width), kv_map(1)),
                  pl.BlockSpec((L, width), kv_map(2)), pl.BlockSpec((L, width), fwd_map(3)),
                  pl.BlockSpec((L, LANES), tab_map), pl.BlockSpec((L, LANES), tab_map),
                  _resident((1, width)), _resident((1, width)), _resident((1, width)),
                  _resident((width, width))],
        out_specs=pl.BlockSpec((L, width), fwd_map(0)),
        out_shape=jax.ShapeDtypeStruct((t, width), BF16),
        scratch_shapes=[pltpu.VMEM((nw, LANES, LANES), F32), pltpu.VMEM((nw, LANES, LANES), F32),
                        pltpu.VMEM((nc, nw, LANES, LANES), F32), pltpu.VMEM((2 * nw, L, L), F32)],
        compiler_params=_params("arbitrary", "arbitrary", "arbitrary"),
        name="retention",
    )(proj, proj, proj, proj, cos_t, sin_t, lgf, lgb, gn, gmat)


def _na_kernel(q_ref, kp_ref, kc_ref, kn_ref, vp_ref, vc_ref, vn_ref, bias_ref, o_ref,
               kbuf, vbuf, *, rows_total):
    i = pl.program_id(1)
    halo = kp_ref.shape[0]
    main = kc_ref.shape[0]
    nkeys = NA_WIN_R * GRID_W
    npair = o_ref.shape[1] // LANES
    kbuf[0:halo] = kp_ref[...]
    kbuf[halo:halo + main] = kc_ref[...]
    kbuf[halo + main:] = kn_ref[...]
    vbuf[0:halo] = vp_ref[...]
    vbuf[halo:halo + main] = vc_ref[...]
    vbuf[halo + main:] = vn_ref[...]
    first = _first_half_mask((1, LANES))
    scale = HEAD_DIM ** -0.5

    def row_body(r, carry):
        gr = i * NA_ROWS_PER_STEP + r
        r0 = jnp.clip(gr - NA_WIN_R // 2, 0, rows_total - NA_WIN_R)
        delta = gr - r0
        off = pl.multiple_of((r0 - (i * NA_ROWS_PER_STEP - NA_WIN_R // 2)) * GRID_W, GRID_W)
        qoff = pl.multiple_of(r * GRID_W, GRID_W)
        qrow = q_ref[pl.ds(qoff, GRID_W), :].astype(F32) * scale
        outs = []
        for pr in range(npair):
            sl = slice(LANES * pr, LANES * (pr + 1))
            kpair = kbuf[pl.ds(off, nkeys), sl]
            vpair = vbuf[pl.ds(off, nkeys), sl]
            qp = qrow[:, sl]
            res = []
            for hh in range(2):
                keep = first if hh == 0 else jnp.logical_not(first)
                qm = jnp.where(keep, qp, 0.0).astype(BF16)
                s = lax.dot_general(qm, kpair, (((1,), (1,)), ((), ())),
                                    preferred_element_type=F32)
                s = s + bias_ref[2 * pr + hh, delta]
                m = jnp.max(s, axis=-1, keepdims=True)
                e = jnp.exp(s - m)
                l = jnp.sum(e, axis=-1, keepdims=True)
                o = jnp.dot(e.astype(BF16), vpair, preferred_element_type=F32)
                res.append(o / l)
            outs.append(jnp.where(first, res[0], res[1]))
        o_ref[pl.ds(qoff, GRID_W), :] = jnp.concatenate(outs, axis=1).astype(o_ref.dtype)
        return carry

    lax.fori_loop(0, NA_ROWS_PER_STEP, row_body, 0)


def _neighborhood_attention(proj, bias, batch, seq, width, col_block0):
    rows = seq // GRID_W
    nstep = rows // NA_ROWS_PER_STEP
    main = NA_ROWS_PER_STEP * GRID_W
    halo = (NA_WIN_R // 2) * GRID_W
    per = main // halo
    nhalo = seq // halo
    t = proj.shape[0]
    qc, kc, vc = col_block0, col_block0 + 1, col_block0 + 2

    def main_map(col):
        return lambda b, i: (b * nstep + i, col)

    def prev_map(col):
        return lambda b, i: (b * nhalo + jnp.maximum(i * per - 1, 0), col)

    def next_map(col):
        return lambda b, i: (b * nhalo + jnp.minimum((i + 1) * per, nhalo - 1), col)

    return pl.pallas_call(
        functools.partial(_na_kernel, rows_total=rows),
        grid=(batch, nstep),
        in_specs=[pl.BlockSpec((main, width), main_map(qc)),
                  pl.BlockSpec((halo, width), prev_map(kc)), pl.BlockSpec((main, width), main_map(kc)),
                  pl.BlockSpec((halo, width), next_map(kc)),
                  pl.BlockSpec((halo, width), prev_map(vc)), pl.BlockSpec((main, width), main_map(vc)),
                  pl.BlockSpec((halo, width), next_map(vc)),
                  _resident(bias.shape)],
        out_specs=pl.BlockSpec((main, width), main_map(0)),
        out_shape=jax.ShapeDtypeStruct((t, width), BF16),
        scratch_shapes=[pltpu.VMEM((main + 2 * halo, width), BF16),
                        pltpu.VMEM((main + 2 * halo, width), BF16)],
        compiler_params=_params("parallel", "parallel"),
        name="neighborhood_attention",
    )(proj, proj, proj, proj, proj, proj, proj, bias)


def _na_bias_table(rpb):
    qc = jnp.arange(GRID_W)
    kc = jnp.arange(GRID_W)
    cstart = jnp.clip(qc - NA_WIN_C // 2, 0, GRID_W - NA_WIN_C)
    valid = (kc[None, :] >= cstart[:, None]) & (kc[None, :] < cstart[:, None] + NA_WIN_C)
    dc = jnp.clip(kc[None, :] - qc[:, None], -(NA_WIN_C - 1), NA_WIN_C - 1) + (NA_WIN_C - 1)
    delta = jnp.arange(NA_WIN_R)
    w = jnp.arange(NA_WIN_R)
    dr = w[None, :] - delta[:, None] + (NA_WIN_R - 1)
    tab = rpb.astype(F32)[:, dr[:, :, None, None], dc[None, None, :, :]]
    tab = jnp.where(valid[None, None, None], tab, NEG_BIG)
    tab = jnp.transpose(tab, (0, 1, 3, 2, 4))
    return tab.reshape(rpb.shape[0], NA_WIN_R, GRID_W, NA_WIN_R * GRID_W)


def _ssd_kernel(x_ref, b_ref, c_ref, dt_ref, z_ref, dtb_ref, alog_ref, dsk_ref, ng_ref,
                o_ref, hf, hb, hbs, *, nc, hpg):
    p = pl.program_id(2)
    c = pl.program_id(3)
    L = x_ref.shape[0]
    npair = x_ref.shape[1] // LANES
    first = _first_half_mask((1, LANES))
    rj = lax.broadcasted_iota(jnp.int32, (L, L), 0)
    rl = lax.broadcasted_iota(jnp.int32, (L, L), 1)
    lower = rj >= rl
    tri = lower.astype(BF16)

    @pl.when((p == 0) & (c == 0))
    def _init():
        hf[...] = jnp.zeros_like(hf)
        hb[...] = jnp.zeros_like(hb)

    raw = dt_ref[...] + dtb_ref[...]
    dt = jnp.maximum(raw, 0.0) + jnp.log1p(jnp.exp(-jnp.abs(raw)))
    a = dt * (-jnp.exp(alog_ref[...]))
    cs = _split3_dot(tri, a)
    cse = cs - a
    cs_t = cs.T
    cse_t = cse.T
    dt_t = dt.T
    tot = cs[L - 1:L, :]
    bt = b_ref[...].astype(F32).T

    def pair_scale(vals):
        return jnp.where(first, vals[0], vals[1])

    @pl.when(p == 0)
    def _backward_states():
        ch = nc - 1 - c
        for pr in range(npair):
            xp = x_ref[:, LANES * pr:LANES * (pr + 1)]
            upd, dec = [], []
            for hh in range(2):
                lane = hpg + 2 * pr + hh
                srow = jnp.exp(cse_t[lane:lane + 1, :]) * dt_t[lane:lane + 1, :]
                upd.append(jnp.dot((bt * srow).astype(BF16), xp, preferred_element_type=F32))
                dec.append(jnp.exp(tot[:, lane:lane + 1]))
            hbs[ch, pr] = hb[pr]
            hb[pr] = hb[pr] * pair_scale(dec) + pair_scale(upd)

    @pl.when(p == 1)
    def _forward():
        cm = c_ref[...]
        cb = lax.dot_general(cm, b_ref[...], (((1,), (1,)), ((), ())), preferred_element_type=F32)
        cf32 = cm.astype(F32)
        ys = []
        for pr in range(npair):
            sl = slice(LANES * pr, LANES * (pr + 1))
            xp = x_ref[:, sl]
            hf_b = hf[pr].astype(BF16)
            hb_b = hbs[c, pr].astype(BF16)
            res, upd, dec = [], [], []
            for hh in range(2):
                lf = 2 * pr + hh
                lb = hpg + lf
                cfc = cs[:, lf:lf + 1]
                cbc = cse[:, lb:lb + 1]
                cfr = cs_t[lf:lf + 1, :]
                cbr = cse_t[lb:lb + 1, :]
                dfr = dt_t[lf:lf + 1, :]
                dbr = dt_t[lb:lb + 1, :]
                e = jnp.where(lower, cfc - cfr, cbr - cbc)
                w = jnp.exp(e) * jnp.where(lower, dfr, dbr)
                m = (cb * w).astype(BF16)
                c_f = (cf32 * jnp.exp(cfc)).astype(BF16)
                c_b = (cf32 * jnp.exp(tot[:, lb:lb + 1] - cbc)).astype(BF16)
                res.append(jnp.dot(m, xp, preferred_element_type=F32)
                           + jnp.dot(c_f, hf_b, preferred_element_type=F32)
                           + jnp.dot(c_b, hb_b, preferred_element_type=F32))
                srow = jnp.exp(tot[:, lf:lf + 1] - cfr) * dfr
                upd.append(jnp.dot((bt * srow).astype(BF16), xp, preferred_element_type=F32))
                dec.append(jnp.exp(tot[:, lf:lf + 1]))
            hf[pr] = hf[pr] * pair_scale(dec) + pair_scale(upd)
            y = pair_scale(res) + xp.astype(F32) * dsk_ref[:, sl]
            z = z_ref[:, sl].astype(F32)
            ys.append(y * (z * jax.nn.sigmoid(z)))
        y = jnp.concatenate(ys, axis=1)
        o_ref[...] = _rms_rows(y, ng_ref[...]).astype(o_ref.dtype)


def _ssd_scan(xbc, dt, z, dtb, alog, dsk, ng, batch, seq, inner, hpg):
    L = SCAN_CHUNK
    nc = seq // L
    gw = hpg * SSD_HEADDIM
    ng_groups = inner // gw
    npair = gw // LANES
    t = xbc.shape[0]
    b_col0 = inner // SSD_STATE
    c_col0 = b_col0 + ng_groups

    def any_map(colf):
        return lambda b, g, p, c: (b * nc + _chunk_of(p, c, nc), colf(g))

    def fwd_map(colf):
        return lambda b, g, p, c: (b * nc + p * c, colf(g))

    grp = lambda b, g, p, c: (0, g)
    return pl.pallas_call(
        functools.partial(_ssd_kernel, nc=nc, hpg=hpg),
        grid=(batch, ng_groups, 2, nc),
        in_specs=[pl.BlockSpec((L, gw), any_map(lambda g: g)),
                  pl.BlockSpec((L, SSD_STATE), any_map(lambda g: b_col0 + g)),
                  pl.BlockSpec((L, SSD_STATE), fwd_map(lambda g: c_col0 + g)),
                  pl.BlockSpec((L, LANES), any_map(lambda g: g)),
                  pl.BlockSpec((L, gw), fwd_map(lambda g: g)),
                  pl.BlockSpec((1, LANES), grp), pl.BlockSpec((1, LANES), grp),
                  pl.BlockSpec((1, gw), grp), pl.BlockSpec((1, gw), grp)],
        out_specs=pl.BlockSpec((L, gw), fwd_map(lambda g: g)),
        out_shape=jax.ShapeDtypeStruct((t, inner), BF16),
        scratch_shapes=[pltpu.VMEM((npair, SSD_STATE, LANES), F32),
                        pltpu.VMEM((npair, SSD_STATE, LANES), F32),
                        pltpu.VMEM((nc, npair, SSD_STATE, LANES), F32)],
        compiler_params=_params("arbitrary", "arbitrary", "arbitrary", "arbitrary"),
        name="ssd_scan",
    )(xbc, xbc, xbc, dt, z, dtb, alog, dsk, ng)


def _rotary_tables(seq):
    half = HEAD_DIM // 2
    inv = 1.0 / (ROPE_BASE ** (jnp.arange(half, dtype=F32) / half))
    ang = jnp.arange(seq).astype(F32)[:, None] * inv[None, :]
    cos = jnp.cos(ang)
    sin = jnp.sin(ang)
    reps = LANES // HEAD_DIM
    cos_t = jnp.tile(jnp.concatenate([cos, cos], axis=1), (1, reps))
    sin_t = jnp.tile(jnp.concatenate([-sin, sin], axis=1), (1, reps))
    return cos_t, sin_t


def _spread_group_lanes(v, hpg):
    lead = v.shape[:-2]
    heads = v.shape[-1]
    groups = heads // hpg
    vg = v.reshape(*lead, 2, groups, hpg)
    vg = jnp.moveaxis(vg, -3, -2).reshape(*lead, groups, 2 * hpg)
    pad = [(0, 0)] * (vg.ndim - 1) + [(0, LANES - 2 * hpg)]
    return jnp.pad(vg, pad).reshape(*lead, groups * LANES)


def kernel(x, norm_mix_pre, norm_mix_post, norm_ffn_pre, norm_ffn_post, ab_w_in, ab_ret_decay_logit, ab_ret_gn_g, ab_na_rpb, ab_w_out, c_w_in, c_conv_w, c_conv_b, c_dt_bias, c_a_log, c_d_skip, c_norm_g, c_w_out, ffn_w_up, ffn_conv_w, ffn_conv_b, ffn_w_down):
    batch, seq, d = x.shape
    depth = norm_mix_pre.shape[0]
    t = batch * seq
    ret_w = RET_HEADS * HEAD_DIM
    na_w = NA_HEADS * HEAD_DIM
    inner = c_w_out.shape[1]
    heads = inner // SSD_HEADDIM
    hpg = heads // SSD_GROUPS
    xbc_w = inner + 2 * SSD_GROUPS * SSD_STATE
    assert ret_w == na_w and ab_w_in.shape[2] == 4 * ret_w + 3 * na_w

    h = x.reshape(t, d).astype(F32)
    cos_t, sin_t = _rotary_tables(seq)
    gmat = jnp.kron(jnp.eye(RET_HEADS, dtype=F32),
                    jnp.full((HEAD_DIM, HEAD_DIM), 1.0 / HEAD_DIM, F32)).astype(BF16)
    row = lambda v: v.reshape(1, -1).astype(F32)

    for layer in range(depth):
        i = layer // 2
        if layer % 2 == 0:
            proj = _norm_proj(h, row(norm_mix_pre[layer]), ab_w_in[i].astype(BF16))
            lg = -jax.nn.softplus(-ab_ret_decay_logit[i].astype(F32))
            lgf = jnp.repeat(lg[0], HEAD_DIM).reshape(1, ret_w)
            lgb = jnp.repeat(lg[1], HEAD_DIM).reshape(1, ret_w)
            ret = _retention(proj, cos_t, sin_t, lgf, lgb, row(ab_ret_gn_g[i]), gmat,
                             batch, seq, ret_w)
            na = _neighborhood_attention(proj, _na_bias_table(ab_na_rpb[i]), batch, seq, na_w,
                                         4 * ret_w // na_w)
            h = _out_proj([ret, na], ab_w_out[i].astype(BF16), row(norm_mix_post[layer]), h)
        else:
            w_in = c_w_in[i]
            wz = w_in[:, :inner].astype(BF16)
            wx = w_in[:, inner:inner + xbc_w].astype(BF16)
            wdt = _spread_group_lanes(w_in[:, inner + xbc_w:].reshape(d, 2, heads), hpg).astype(BF16)
            z, xbc, dtr = _ssd_inproj(h, row(norm_mix_pre[layer]), wz, wx, wdt,
                                      c_conv_w[i].astype(F32), row(c_conv_b[i]), seq)
            dtb = _spread_group_lanes(c_dt_bias[i].astype(F32), hpg).reshape(1, -1)
            alog = _spread_group_lanes(c_a_log[i].astype(F32), hpg).reshape(1, -1)
            dsk = jnp.repeat(c_d_skip[i].astype(F32), SSD_HEADDIM).reshape(1, inner)
            y = _ssd_scan(xbc, dtr, z, dtb, alog, dsk, row(c_norm_g[i]), batch, seq, inner, hpg)
            h = _out_proj([y], c_w_out[i].astype(BF16), row(norm_mix_post[layer]), h)
        h = _ffn(h, row(norm_ffn_pre[layer]), ffn_w_up[layer].astype(BF16),
                 ffn_conv_w[layer].astype(F32), row(ffn_conv_b[layer]),
                 ffn_w_down[layer].astype(BF16), row(norm_ffn_post[layer]), seq)
    return h.reshape(batch, seq, d).astype(x.dtype)
```

```python
import functools
import math

import jax
import jax.numpy as jnp
from jax import lax
from jax.experimental import pallas as pl
from jax.experimental.pallas import tpu as pltpu

F32 = jnp.float32
BF16 = jnp.bfloat16

EPS = 1e-6
GRID_W = 64
ROPE_BASE = 10000.0
RET_HEADS = 8
NA_HEADS = 8
NA_WIN_R = 8
NA_WIN_C = 16
SSD_HEADDIM = 64
SSD_GROUPS = 4
SSD_STATE = 128
HEAD_DIM = 64
LANES = 128
HALO = 8
SCAN_CHUNK = 128
NA_ROWS_PER_STEP = 16
NEG_BIG = -1e30
VMEM_LIMIT = 56 * 1024 * 1024


def _params(*sem):
    return pltpu.CompilerParams(dimension_semantics=sem, vmem_limit_bytes=VMEM_LIMIT)


def _resident(shape):
    nd = len(shape)
    return pl.BlockSpec(shape, lambda *_: (0,) * nd, pipeline_mode=pl.Buffered(1))


def _rms_rows(x, g):
    return x * lax.rsqrt(jnp.mean(x * x, axis=-1, keepdims=True) + EPS) * g


def _split3_dot(tri, a):
    hi = a.astype(BF16)
    r1 = a - hi.astype(F32)
    mid = r1.astype(BF16)
    lo = (r1 - mid.astype(F32)).astype(BF16)
    return (jnp.dot(tri, hi, preferred_element_type=F32)
            + jnp.dot(tri, mid, preferred_element_type=F32)
            + jnp.dot(tri, lo, preferred_element_type=F32))


def _split2_dot(a, m):
    hi = a.astype(BF16)
    lo = (a - hi.astype(F32)).astype(BF16)
    return jnp.dot(hi, m, preferred_element_type=F32) + jnp.dot(lo, m, preferred_element_type=F32)


def _norm_proj_kernel(x_ref, g_ref, w_ref, o_ref, *, ncol):
    hn = _rms_rows(x_ref[...], g_ref[...]).astype(BF16)
    for c in range(0, w_ref.shape[1], ncol):
        o_ref[:, c:c + ncol] = jnp.dot(hn, w_ref[:, c:c + ncol],
                                       preferred_element_type=F32).astype(o_ref.dtype)


def _norm_proj(x, g, w, tm=512, ncol=512):
    t, d = x.shape
    n = w.shape[1]
    return pl.pallas_call(
        functools.partial(_norm_proj_kernel, ncol=ncol),
        grid=(t // tm,),
        in_specs=[pl.BlockSpec((tm, d), lambda i: (i, 0)), _resident((1, d)), _resident((d, n))],
        out_specs=pl.BlockSpec((tm, n), lambda i: (i, 0)),
        out_shape=jax.ShapeDtypeStruct((t, n), BF16),
        compiler_params=_params("parallel"),
        name="norm_proj",
    )(x, g, w)


def _out_proj_kernel(*refs, n_act):
    acts = refs[:n_act]
    w_ref, g_ref, x_ref, o_ref = refs[n_act:]
    k0 = 0
    m = None
    for a in acts:
        k = a.shape[1]
        part = jnp.dot(a[...], w_ref[k0:k0 + k, :], preferred_element_type=F32)
        m = part if m is None else m + part
        k0 += k
    o_ref[...] = x_ref[...] + _rms_rows(m, g_ref[...])


def _out_proj(acts, w, g, x, tm=512):
    t, d = x.shape
    in_specs = [pl.BlockSpec((tm, a.shape[1]), lambda i: (i, 0)) for a in acts]
    in_specs += [_resident(w.shape), _resident((1, d)), pl.BlockSpec((tm, d), lambda i: (i, 0))]
    return pl.pallas_call(
        functools.partial(_out_proj_kernel, n_act=len(acts)),
        grid=(t // tm,),
        in_specs=in_specs,
        out_specs=pl.BlockSpec((tm, d), lambda i: (i, 0)),
        out_shape=jax.ShapeDtypeStruct((t, d), F32),
        compiler_params=_params("parallel"),
        name="out_proj",
    )(*acts, w, g, x)


def _halo_specs(tm, d, t):
    nb = tm // HALO
    last = t // HALO - 1
    return [
        pl.BlockSpec((HALO, d), lambda i: (jnp.maximum(i * nb - 1, 0), 0)),
        pl.BlockSpec((tm, d), lambda i: (i, 0)),
        pl.BlockSpec((HALO, d), lambda i: (jnp.minimum((i + 1) * nb, last), 0)),
    ]


def _normed_tile_with_halo(xp_ref, x_ref, xn_ref, g_ref, tiles_per_seq):
    i = pl.program_id(0)
    pos = i % tiles_per_seq
    has_prev = (pos != 0).astype(F32)
    has_next = (pos != tiles_per_seq - 1).astype(F32)
    g = g_ref[...]
    top = _rms_rows(xp_ref[...], g) * has_prev
    mid = _rms_rows(x_ref[...], g)
    bot = _rms_rows(xn_ref[...], g) * has_next
    return jnp.concatenate([top, mid, bot], axis=0).astype(BF16)


def _depthwise_conv_rows(u, w_ref, b_ref, cols, tm):
    n = u.shape[0]
    width = w_ref.shape[0]
    pad = width // 2
    acc = None
    for k in range(width):
        shift = pad - k
        us = u if shift == 0 else pltpu.roll(u, shift % n, 0)
        term = us[HALO:HALO + tm] * w_ref[k:k + 1, cols]
        acc = term if acc is None else acc + term
    return acc + b_ref[:, cols]


def _ffn_kernel(xp_ref, x_ref, xn_ref, gpre_ref, wup_ref, cw_ref, cb_ref, wdn_ref, gpost_ref,
                o_ref, *, tiles_per_seq, ncol):
    tm = x_ref.shape[0]
    f = wdn_ref.shape[0]
    hn = _normed_tile_with_halo(xp_ref, x_ref, xn_ref, gpre_ref, tiles_per_seq)
    acc = None
    for c in range(0, f, ncol):
        gcols = slice(c, c + ncol)
        vcols = slice(f + c, f + c + ncol)
        ug = jnp.dot(hn, wup_ref[:, gcols], preferred_element_type=F32)
        uv = jnp.dot(hn, wup_ref[:, vcols], preferred_element_type=F32)
        gate = _depthwise_conv_rows(ug, cw_ref, cb_ref, gcols, tm)
        val = _depthwise_conv_rows(uv, cw_ref, cb_ref, vcols, tm)
        h = (jax.nn.gelu(gate, approximate=True) * val).astype(BF16)
        part = jnp.dot(h, wdn_ref[gcols, :], preferred_element_type=F32)
        acc = part if acc is None else acc + part
    o_ref[...] = x_ref[...] + _rms_rows(acc, gpost_ref[...])


def _ffn(x, gpre, wup, cw, cb, wdn, gpost, seq, tm=512, ncol=256):
    t, d = x.shape
    f = wdn.shape[0]
    return pl.pallas_call(
        functools.partial(_ffn_kernel, tiles_per_seq=seq // tm, ncol=ncol),
        grid=(t // tm,),
        in_specs=_halo_specs(tm, d, t) + [
            _resident((1, d)), _resident(wup.shape), _resident(cw.shape), _resident((1, 2 * f)),
            _resident(wdn.shape), _resident((1, d))],
        out_specs=pl.BlockSpec((tm, d), lambda i: (i, 0)),
        out_shape=jax.ShapeDtypeStruct((t, d), F32),
        compiler_params=_params("parallel"),
        name="ffn",
    )(x, x, x, gpre, wup, cw, cb, wdn, gpost)


def _ssd_inproj_kernel(xp_ref, x_ref, xn_ref, g_ref, wz_ref, wx_ref, wdt_ref, cw_ref, cb_ref,
                       z_ref, xbc_ref, dt_ref, *, tiles_per_seq, ncol):
    tm = x_ref.shape[0]
    hn = _normed_tile_with_halo(xp_ref, x_ref, xn_ref, g_ref, tiles_per_seq)
    hm = hn[HALO:HALO + tm]
    for c in range(0, wz_ref.shape[1], ncol):
        z_ref[:, c:c + ncol] = jnp.dot(hm, wz_ref[:, c:c + ncol],
                                       preferred_element_type=F32).astype(z_ref.dtype)
    dt_ref[...] = jnp.dot(hm, wdt_ref[...], preferred_element_type=F32)
    for c in range(0, wx_ref.shape[1], ncol):
        cols = slice(c, c + ncol)
        u = jnp.dot(hn, wx_ref[:, cols], preferred_element_type=F32)
        v = _depthwise_conv_rows(u, cw_ref, cb_ref, cols, tm)
        xbc_ref[:, cols] = (v * jax.nn.sigmoid(v)).astype(xbc_ref.dtype)


def _ssd_inproj(x, g, wz, wx, wdt, cw, cb, seq, tm=512, ncol=512):
    t, d = x.shape
    nz, nx, ndt = wz.shape[1], wx.shape[1], wdt.shape[1]
    row = lambda n: pl.BlockSpec((tm, n), lambda i: (i, 0))
    return pl.pallas_call(
        functools.partial(_ssd_inproj_kernel, tiles_per_seq=seq // tm, ncol=ncol),
        grid=(t // tm,),
        in_specs=_halo_specs(tm, d, t) + [
            _resident((1, d)), _resident(wz.shape), _resident(wx.shape), _resident(wdt.shape),
            _resident(cw.shape), _resident((1, nx))],
        out_specs=[row(nz), row(nx), row(ndt)],
        out_shape=[jax.ShapeDtypeStruct((t, nz), BF16), jax.ShapeDtypeStruct((t, nx), BF16),
                   jax.ShapeDtypeStruct((t, ndt), F32)],
        compiler_params=_params("parallel"),
        name="ssd_inproj",
    )(x, x, x, g, wz, wx, wdt, cw, cb)


def _chunk_of(p, c, nc):
    return p * c + (1 - p) * (nc - 1 - c)


def _first_half_mask(shape):
    return lax.broadcasted_iota(jnp.int32, shape, len(shape) - 1) < HEAD_DIM


def _ret_kernel(q_ref, k_ref, v_ref, g_ref, cos_ref, sin_ref, lgf_ref, lgb_ref, gn_ref, gm_ref,
                o_ref, hf, hb, hbs, dm, *, nc):
    p = pl.program_id(1)
    c = pl.program_id(2)
    L = k_ref.shape[0]
    npair = k_ref.shape[1] // LANES
    lgf = lgf_ref[...]
    lgb = lgb_ref[...]
    first = _first_half_mask((1, LANES))
    swap_up = (lax.broadcasted_iota(jnp.int32, (1, LANES), 1) % HEAD_DIM) < HEAD_DIM // 2
    ri = lax.broadcasted_iota(jnp.int32, (LANES, LANES), 0) < HEAD_DIM
    ci = lax.broadcasted_iota(jnp.int32, (LANES, LANES), 1) < HEAD_DIM
    same_head = ri == ci
    jj = lax.broadcasted_iota(jnp.int32, (L, 1), 0).astype(F32)
    cos_f = cos_ref[...]
    sin_s = sin_ref[...]
    scale = HEAD_DIM ** -0.5

    def rot(x):
        sw = jnp.where(swap_up, pltpu.roll(x, LANES - HEAD_DIM // 2, 1),
                       pltpu.roll(x, HEAD_DIM // 2, 1))
        return x * cos_f + sw * sin_s

    @pl.when((p == 0) & (c == 0))
    def _init():
        hf[...] = jnp.zeros_like(hf)
        hb[...] = jnp.zeros_like(hb)
        d = (lax.broadcasted_iota(jnp.int32, (L, L), 0)
             - lax.broadcasted_iota(jnp.int32, (L, L), 1)).astype(F32)
        for h in range(2 * npair):
            lf = lgf[:, HEAD_DIM * h:HEAD_DIM * h + 1]
            lb = lgb[:, HEAD_DIM * h:HEAD_DIM * h + 1]
            dm[h] = jnp.exp(jnp.where(d >= 0, d * lf, -d * lb))

    @pl.when(p == 0)
    def _backward_states():
        ch = nc - 1 - c
        k = k_ref[...].astype(F32)
        tail_b = jnp.exp(jj * lgb)
        decay_b = jnp.exp(L * lgb)
        for pr in range(npair):
            sl = slice(LANES * pr, LANES * (pr + 1))
            kp = rot(k[:, sl]) * scale
            kt = (kp * tail_b[:, sl]).T.astype(BF16)
            hbs[ch, pr] = hb[pr]
            upd = jnp.dot(kt, v_ref[:, sl], preferred_element_type=F32)
            hb[pr] = jnp.where(same_head, hb[pr] * decay_b[:, sl] + upd, 0.0)

    @pl.when(p == 1)
    def _forward():
        q = q_ref[...].astype(F32)
        k = k_ref[...].astype(F32)
        dec_f = jnp.exp((jj + 1.0) * lgf)
        dec_b = jnp.exp((L - jj) * lgb)
        tail_f = jnp.exp((L - 1.0 - jj) * lgf)
        decay_f = jnp.exp(L * lgf)
        ys = []
        for pr in range(npair):
            sl = slice(LANES * pr, LANES * (pr + 1))
            qp = rot(q[:, sl])
            kp = rot(k[:, sl]) * scale
            kb = kp.astype(BF16)
            vp = v_ref[:, sl]
            res = []
            for hh in range(2):
                keep = first if hh == 0 else jnp.logical_not(first)
                qm = jnp.where(keep, qp, 0.0).astype(BF16)
                s = lax.dot_general(qm, kb, (((1,), (1,)), ((), ())), preferred_element_type=F32)
                a = (s * dm[2 * pr + hh]).astype(BF16)
                res.append(jnp.dot(a, vp, preferred_element_type=F32))
            y = jnp.where(first, res[0], res[1])
            y = y + jnp.dot((qp * dec_f[:, sl]).astype(BF16), hf[pr].astype(BF16),
                            preferred_element_type=F32)
            y = y + jnp.dot((qp * dec_b[:, sl]).astype(BF16), hbs[c, pr].astype(BF16),
                            preferred_element_type=F32)
            kt = (kp * tail_f[:, sl]).T.astype(BF16)
            upd = jnp.dot(kt, vp, preferred_element_type=F32)
            hf[pr] = jnp.where(same_head, hf[pr] * decay_f[:, sl] + upd, 0.0)
            ys.append(y)
        y = jnp.concatenate(ys, axis=1)
        gm = gm_ref[...]
        mu = _split2_dot(y, gm)
        d = y - mu
        var = _split2_dot(d * d, gm)
        yn = d * lax.rsqrt(var + EPS) * gn_ref[...]
        g = g_ref[...].astype(F32)
        o_ref[...] = (g * jax.nn.sigmoid(g) * yn).astype(o_ref.dtype)


def _retention(proj, cos_t, sin_t, lgf, lgb, gn, gmat, batch, seq, width):
    L = SCAN_CHUNK
    nc = seq // L
    nw = width // LANES
    t = proj.shape[0]

    def kv_map(col):
        return lambda b, p, c: (b * nc + _chunk_of(p, c, nc), col)

    def fwd_map(col):
        return lambda b, p, c: (b * nc + p * c, col)

    tab_map = lambda b, p, c: (_chunk_of(p, c, nc), 0)
    return pl.pallas_call(
        functools.partial(_ret_kernel, nc=nc),
        grid=(batch, 2, nc),
        in_specs=[pl.BlockSpec((L, width), fwd_map(0)), pl.BlockSpec((L, width), kv_map(1)),
                  pl.BlockSpec((L, width), kv_map(2)), pl.BlockSpec((L, width), fwd_map(3)),
                  pl.BlockSpec((L, LANES), tab_map), pl.BlockSpec((L, LANES), tab_map),
                  _resident((1, width)), _resident((1, width)), _resident((1, width)),
                  _resident((width, width))],
        out_specs=pl.BlockSpec((L, width), fwd_map(0)),
        out_shape=jax.ShapeDtypeStruct((t, width), BF16),
        scratch_shapes=[pltpu.VMEM((nw, LANES, LANES), F32), pltpu.VMEM((nw, LANES, LANES), F32),
                        pltpu.VMEM((nc, nw, LANES, LANES), F32), pltpu.VMEM((2 * nw, L, L), F32)],
        compiler_params=_params("arbitrary", "arbitrary", "arbitrary"),
        name="retention",
    )(proj, proj, proj, proj, cos_t, sin_t, lgf, lgb, gn, gmat)


def _na_kernel(q_ref, kp_ref, kc_ref, kn_ref, vp_ref, vc_ref, vn_ref, bias_ref, o_ref,
               kbuf, vbuf, *, rows_total):
    i = pl.program_id(1)
    halo = kp_ref.shape[0]
    main = kc_ref.shape[0]
    nkeys = NA_WIN_R * GRID_W
    npair = o_ref.shape[1] // LANES
    kbuf[0:halo] = kp_ref[...]
    kbuf[halo:halo + main] = kc_ref[...]
    kbuf[halo + main:] = kn_ref[...]
    vbuf[0:halo] = vp_ref[...]
    vbuf[halo:halo + main] = vc_ref[...]
    vbuf[halo + main:] = vn_ref[...]
    first = _first_half_mask((1, LANES))
    scale = HEAD_DIM ** -0.5

    def row_body(r, carry):
        gr = i * NA_ROWS_PER_STEP + r
        r0 = jnp.clip(gr - NA_WIN_R // 2, 0, rows_total - NA_WIN_R)
        delta = gr - r0
        off = pl.multiple_of((r0 - (i * NA_ROWS_PER_STEP - NA_WIN_R // 2)) * GRID_W, GRID_W)
        qoff = pl.multiple_of(r * GRID_W, GRID_W)
        qrow = q_ref[pl.ds(qoff, GRID_W), :].astype(F32) * scale
        outs = []
        for pr in range(npair):
            sl = slice(LANES * pr, LANES * (pr + 1))
            kpair = kbuf[pl.ds(off, nkeys), sl]
            vpair = vbuf[pl.ds(off, nkeys), sl]
            qp = qrow[:, sl]
            res = []
            for hh in range(2):
                keep = first if hh == 0 else jnp.logical_not(first)
                qm = jnp.where(keep, qp, 0.0).astype(BF16)
                s = lax.dot_general(qm, kpair, (((1,), (1,)), ((), ())),
                                    preferred_element_type=F32)
                s = s + bias_ref[2 * pr + hh, delta]
                m = jnp.max(s, axis=-1, keepdims=True)
                e = jnp.exp(s - m)
                l = jnp.sum(e, axis=-1, keepdims=True)
                o = jnp.dot(e.astype(BF16), vpair, preferred_element_type=F32)
                res.append(o / l)
            outs.append(jnp.where(first, res[0], res[1]))
        o_ref[pl.ds(qoff, GRID_W), :] = jnp.concatenate(outs, axis=1).astype(o_ref.dtype)
        return carry

    lax.fori_loop(0, NA_ROWS_PER_STEP, row_body, 0)


def _neighborhood_attention(proj, bias, batch, seq, width, col_block0):
    rows = seq // GRID_W
    nstep = rows // NA_ROWS_PER_STEP
    main = NA_ROWS_PER_STEP * GRID_W
    halo = (NA_WIN_R // 2) * GRID_W
    per = main // halo
    nhalo = seq // halo
    t = proj.shape[0]
    qc, kc, vc = col_block0, col_block0 + 1, col_block0 + 2

    def main_map(col):
        return lambda b, i: (b * nstep + i, col)

    def prev_map(col):
        return lambda b, i: (b * nhalo + jnp.maximum(i * per - 1, 0), col)

    def next_map(col):
        return lambda b, i: (b * nhalo + jnp.minimum((i + 1) * per, nhalo - 1), col)

    return pl.pallas_call(
        functools.partial(_na_kernel, rows_total=rows),
        grid=(batch, nstep),
        in_specs=[pl.BlockSpec((main, width), main_map(qc)),
                  pl.BlockSpec((halo, width), prev_map(kc)), pl.BlockSpec((main, width), main_map(kc)),
                  pl.BlockSpec((halo, width), next_map(kc)),
                  pl.BlockSpec((halo, width), prev_map(vc)), pl.BlockSpec((main, width), main_map(vc)),
                  pl.BlockSpec((halo, width), next_map(vc)),
                  _resident(bias.shape)],
        out_specs=pl.BlockSpec((main, width), main_map(0)),
        out_shape=jax.ShapeDtypeStruct((t, width), BF16),
        scratch_shapes=[pltpu.VMEM((main + 2 * halo, width), BF16),
                        pltpu.VMEM((main + 2 * halo, width), BF16)],
        compiler_params=_params("parallel", "parallel"),
        name="neighborhood_attention",
    )(proj, proj, proj, proj, proj, proj, proj, bias)


def _na_bias_table(rpb):
    qc = jnp.arange(GRID_W)
    kc = jnp.arange(GRID_W)
    cstart = jnp.clip(qc - NA_WIN_C // 2, 0, GRID_W - NA_WIN_C)
    valid = (kc[None, :] >= cstart[:, None]) & (kc[None, :] < cstart[:, None] + NA_WIN_C)
    dc = jnp.clip(kc[None, :] - qc[:, None], -(NA_WIN_C - 1), NA_WIN_C - 1) + (NA_WIN_C - 1)
    onehot = (dc[None] == jnp.arange(2 * NA_WIN_C - 1)[:, None, None]).astype(F32)
    toep = jnp.einsum('hrd,dqk->hrqk', rpb.astype(F32), onehot, precision=lax.Precision.HIGHEST)
    toep = jnp.where(valid[None, None], toep, NEG_BIG)
    per_delta = [toep[:, NA_WIN_R - 1 - dl:2 * NA_WIN_R - 1 - dl] for dl in range(NA_WIN_R)]
    tab = jnp.stack(per_delta, axis=1)
    tab = jnp.transpose(tab, (0, 1, 3, 2, 4))
    return tab.reshape(rpb.shape[0], NA_WIN_R, GRID_W, NA_WIN_R * GRID_W)


def _ssd_kernel(x_ref, b_ref, c_ref, dt_ref, z_ref, dtb_ref, alog_ref, dsk_ref, ng_ref,
                o_ref, hf, hb, hbs, *, nc, hpg, groups):
    p = pl.program_id(1)
    c = pl.program_id(2)
    L = x_ref.shape[0]
    assert L == SSD_STATE == LANES
    gw = hpg * SSD_HEADDIM
    npair = gw // LANES
    first = _first_half_mask((1, LANES))
    rj = lax.broadcasted_iota(jnp.int32, (L, L), 0)
    rl = lax.broadcasted_iota(jnp.int32, (L, L), 1)
    lower = rj >= rl
    tri = lower.astype(BF16)

    @pl.when((p == 0) & (c == 0))
    def _init():
        hf[...] = jnp.zeros_like(hf)
        hb[...] = jnp.zeros_like(hb)

    raw = dt_ref[...] + dtb_ref[...]
    dt = jnp.maximum(raw, 0.0) + jnp.log1p(jnp.exp(-jnp.abs(raw)))
    a = dt * (-jnp.exp(alog_ref[...]))
    cs = _split3_dot(tri, a)
    cse = cs - a
    cs_t = cs.T
    cse_t = cse.T
    dt_t = dt.T
    tot = cs[L - 1:L, :]

    def pair_scale(vals):
        return jnp.where(first, vals[0], vals[1])

    def row(m, lane):
        return m[lane:lane + 1, :]

    def col_bcast(m, lane):
        return jnp.broadcast_to(m[:, lane:lane + 1], (L, LANES))

    @pl.when(p == 0)
    def _backward_states():
        ch = nc - 1 - c
        for g in range(groups):
            bt = b_ref[:, SSD_STATE * g:SSD_STATE * (g + 1)].astype(F32).T
            for pr in range(npair):
                idx = g * npair + pr
                xp = x_ref[:, LANES * idx:LANES * (idx + 1)]
                upd, dec = [], []
                for hh in range(2):
                    lb = 2 * hpg * g + hpg + 2 * pr + hh
                    srow = jnp.exp(row(cse_t, lb)) * row(dt_t, lb)
                    upd.append(jnp.dot((bt * srow).astype(BF16), xp, preferred_element_type=F32))
                    dec.append(jnp.exp(tot[:, lb:lb + 1]))
                hbs[ch, idx] = hb[idx].astype(BF16)
                hb[idx] = hb[idx] * pair_scale(dec) + pair_scale(upd)

    @pl.when(p == 1)
    def _forward():
        for g in range(groups):
            bm = b_ref[:, SSD_STATE * g:SSD_STATE * (g + 1)]
            cm = c_ref[:, SSD_STATE * g:SSD_STATE * (g + 1)]
            bt = bm.astype(F32).T
            cb = lax.dot_general(cm, bm, (((1,), (1,)), ((), ())), preferred_element_type=F32)
            cf32 = cm.astype(F32)
            ys = []
            for pr in range(npair):
                idx = g * npair + pr
                sl = slice(LANES * idx, LANES * (idx + 1))
                xp = x_ref[:, sl]
                hf_b = hf[idx].astype(BF16)
                hb_b = hbs[c, idx]
                res, upd, dec = [], [], []
                for hh in range(2):
                    lf = 2 * hpg * g + 2 * pr + hh
                    lb = lf + hpg
                    cfc = col_bcast(cs, lf)
                    cbc = col_bcast(cse, lb)
                    cfr = row(cs_t, lf)
                    e = jnp.where(lower, cfc - cfr, row(cse_t, lb) - cbc)
                    w = jnp.exp(e) * jnp.where(lower, row(dt_t, lf), row(dt_t, lb))
                    m = (cb * w).astype(BF16)
                    c_f = (cf32 * jnp.exp(cfc)).astype(BF16)
                    c_b = (cf32 * jnp.exp(tot[:, lb:lb + 1] - cbc)).astype(BF16)
                    res.append(jnp.dot(m, xp, preferred_element_type=F32)
                               + jnp.dot(c_f, hf_b, preferred_element_type=F32)
                               + jnp.dot(c_b, hb_b, preferred_element_type=F32))
                    srow = jnp.exp(tot[:, lf:lf + 1] - cfr) * row(dt_t, lf)
                    upd.append(jnp.dot((bt * srow).astype(BF16), xp, preferred_element_type=F32))
                    dec.append(jnp.exp(tot[:, lf:lf + 1]))
                hf[idx] = hf[idx] * pair_scale(dec) + pair_scale(upd)
                y = pair_scale(res) + xp.astype(F32) * dsk_ref[:, sl]
                z = z_ref[:, sl].astype(F32)
                ys.append(y * (z * jax.nn.sigmoid(z)))
            y = jnp.concatenate(ys, axis=1)
            gsl = slice(gw * g, gw * (g + 1))
            o_ref[:, gsl] = _rms_rows(y, ng_ref[:, gsl]).astype(o_ref.dtype)


def _ssd_scan(xbc, dt, z, dtb, alog, dsk, ng, batch, seq, inner, hpg):
    L = SCAN_CHUNK
    nc = seq // L
    groups = inner // (hpg * SSD_HEADDIM)
    npairs = inner // LANES
    bc_w = groups * SSD_STATE
    t = xbc.shape[0]
    b_col = inner // bc_w

    def any_map(col):
        return lambda b, p, c: (b * nc + _chunk_of(p, c, nc), col)

    def fwd_map(col):
        return lambda b, p, c: (b * nc + p * c, col)

    return pl.pallas_call(
        functools.partial(_ssd_kernel, nc=nc, hpg=hpg, groups=groups),
        grid=(batch, 2, nc),
        in_specs=[pl.BlockSpec((L, inner), any_map(0)),
                  pl.BlockSpec((L, bc_w), any_map(b_col)),
                  pl.BlockSpec((L, bc_w), fwd_map(b_col + 1)),
                  pl.BlockSpec((L, LANES), any_map(0)),
                  pl.BlockSpec((L, inner), fwd_map(0)),
                  _resident((1, LANES)), _resident((1, LANES)),
                  _resident((1, inner)), _resident((1, inner))],
        out_specs=pl.BlockSpec((L, inner), fwd_map(0)),
        out_shape=jax.ShapeDtypeStruct((t, inner), BF16),
        scratch_shapes=[pltpu.VMEM((npairs, SSD_STATE, LANES), F32),
                        pltpu.VMEM((npairs, SSD_STATE, LANES), F32),
                        pltpu.VMEM((nc, npairs, SSD_STATE, LANES), BF16)],
        compiler_params=_params("arbitrary", "arbitrary", "arbitrary"),
        name="ssd_scan",
    )(xbc, xbc, xbc, dt, z, dtb, alog, dsk, ng)


def _rotary_tables(seq):
    half = HEAD_DIM // 2
    inv = 1.0 / (ROPE_BASE ** (jnp.arange(half, dtype=F32) / half))
    ang = jnp.arange(seq).astype(F32)[:, None] * inv[None, :]
    cos = jnp.cos(ang)
    sin = jnp.sin(ang)
    reps = LANES // HEAD_DIM
    cos_t = jnp.tile(jnp.concatenate([cos, cos], axis=1), (1, reps))
    sin_t = jnp.tile(jnp.concatenate([-sin, sin], axis=1), (1, reps))
    return cos_t, sin_t


def _group_dt_lanes(v, hpg):
    lead = v.shape[:-2]
    heads = v.shape[-1]
    groups = heads // hpg
    assert 2 * heads <= LANES
    vg = v.reshape(*lead, 2, groups, hpg)
    vg = jnp.moveaxis(vg, -3, -2).reshape(*lead, 2 * heads)
    pad = [(0, 0)] * (vg.ndim - 1) + [(0, LANES - 2 * heads)]
    return jnp.pad(vg, pad)


def kernel(x, norm_mix_pre, norm_mix_post, norm_ffn_pre, norm_ffn_post, ab_w_in, ab_ret_decay_logit, ab_ret_gn_g, ab_na_rpb, ab_w_out, c_w_in, c_conv_w, c_conv_b, c_dt_bias, c_a_log, c_d_skip, c_norm_g, c_w_out, ffn_w_up, ffn_conv_w, ffn_conv_b, ffn_w_down):
    batch, seq, d = x.shape
    depth = norm_mix_pre.shape[0]
    t = batch * seq
    ret_w = RET_HEADS * HEAD_DIM
    na_w = NA_HEADS * HEAD_DIM
    inner = c_w_out.shape[1]
    heads = inner // SSD_HEADDIM
    hpg = heads // SSD_GROUPS
    xbc_w = inner + 2 * SSD_GROUPS * SSD_STATE
    assert ret_w == na_w and ab_w_in.shape[2] == 4 * ret_w + 3 * na_w

    h = x.reshape(t, d).astype(F32)
    cos_t, sin_t = _rotary_tables(seq)
    gmat = jnp.kron(jnp.eye(RET_HEADS, dtype=F32),
                    jnp.full((HEAD_DIM, HEAD_DIM), 1.0 / HEAD_DIM, F32)).astype(BF16)
    row = lambda v: v.reshape(1, -1).astype(F32)

    for layer in range(depth):
        i = layer // 2
        if layer % 2 == 0:
            proj = _norm_proj(h, row(norm_mix_pre[layer]), ab_w_in[i].astype(BF16))
            lg = -jax.nn.softplus(-ab_ret_decay_logit[i].astype(F32))
            lgf = jnp.repeat(lg[0], HEAD_DIM).reshape(1, ret_w)
            lgb = jnp.repeat(lg[1], HEAD_DIM).reshape(1, ret_w)
            ret = _retention(proj, cos_t, sin_t, lgf, lgb, row(ab_ret_gn_g[i]), gmat,
                             batch, seq, ret_w)
            na = _neighborhood_attention(proj, _na_bias_table(ab_na_rpb[i]), batch, seq, na_w,
                                         4 * ret_w // na_w)
            h = _out_proj([ret, na], ab_w_out[i].astype(BF16), row(norm_mix_post[layer]), h)
        else:
            w_in = c_w_in[i]
            wz = w_in[:, :inner].astype(BF16)
            wx = w_in[:, inner:inner + xbc_w].astype(BF16)
            wdt = _group_dt_lanes(w_in[:, inner + xbc_w:].reshape(d, 2, heads), hpg).astype(BF16)
            z, xbc, dtr = _ssd_inproj(h, row(norm_mix_pre[layer]), wz, wx, wdt,
                                      c_conv_w[i].astype(F32), row(c_conv_b[i]), seq)
            dtb = _group_dt_lanes(c_dt_bias[i].astype(F32), hpg).reshape(1, -1)
            alog = _group_dt_lanes(c_a_log[i].astype(F32), hpg).reshape(1, -1)
            dsk = jnp.repeat(c_d_skip[i].astype(F32), SSD_HEADDIM).reshape(1, inner)
            y = _ssd_scan(xbc, dtr, z, dtb, alog, dsk, row(c_norm_g[i]), batch, seq, inner, hpg)
            h = _out_proj([y], c_w_out[i].astype(BF16), row(norm_mix_post[layer]), h)
        h = _ffn(h, row(norm_ffn_pre[layer]), ffn_w_up[layer].astype(BF16),
                 ffn_conv_w[layer].astype(F32), row(ffn_conv_b[layer]),
                 ffn_w_down[layer].astype(BF16), row(norm_ffn_post[layer]), seq)
    return h.reshape(batch, seq, d).astype(x.dtype)
```

```python
import functools
import math

import jax
import jax.numpy as jnp
from jax import lax
from jax.experimental import pallas as pl
from jax.experimental.pallas import tpu as pltpu

F32 = jnp.float32
BF16 = jnp.bfloat16

EPS = 1e-6
GRID_W = 64
ROPE_BASE = 10000.0
RET_HEADS = 8
NA_HEADS = 8
NA_WIN_R = 8
NA_WIN_C = 16
SSD_HEADDIM = 64
SSD_GROUPS = 4
SSD_STATE = 128
HEAD_DIM = 64
LANES = 128
HALO = 8
SCAN_CHUNK = 128
NA_ROWS_PER_STEP = 32
NA_ROW_UNROLL = 1
SSD_INPROJ_CONV_COLS = 1024
NEG_BIG = -1e30
VMEM_LIMIT = 56 * 1024 * 1024


def _params(*sem):
    return pltpu.CompilerParams(dimension_semantics=sem, vmem_limit_bytes=VMEM_LIMIT)


def _resident(shape):
    nd = len(shape)
    return pl.BlockSpec(shape, lambda *_: (0,) * nd, pipeline_mode=pl.Buffered(1))


def _rms_rows(x, g):
    return x * lax.rsqrt(jnp.mean(x * x, axis=-1, keepdims=True) + EPS) * g


def _split3_dot(tri, a):
    hi = a.astype(BF16)
    r1 = a - hi.astype(F32)
    mid = r1.astype(BF16)
    lo = (r1 - mid.astype(F32)).astype(BF16)
    return (jnp.dot(tri, hi, preferred_element_type=F32)
            + jnp.dot(tri, mid, preferred_element_type=F32)
            + jnp.dot(tri, lo, preferred_element_type=F32))


def _split2_dot(a, m):
    hi = a.astype(BF16)
    lo = (a - hi.astype(F32)).astype(BF16)
    return jnp.dot(hi, m, preferred_element_type=F32) + jnp.dot(lo, m, preferred_element_type=F32)


def _norm_proj_kernel(x_ref, g_ref, w_ref, wt_ref, o_ref, ot_ref):
    hn = _rms_rows(x_ref[...], g_ref[...]).astype(BF16)
    o_ref[...] = jnp.dot(hn, w_ref[...], preferred_element_type=F32).astype(o_ref.dtype)
    ot_ref[...] = lax.dot_general(wt_ref[...], hn, (((1,), (1,)), ((), ())),
                                  preferred_element_type=F32).astype(ot_ref.dtype)


def _norm_proj(x, g, w, wt, tm=512):
    t, d = x.shape
    n = w.shape[1]
    nt = wt.shape[0]
    return pl.pallas_call(
        _norm_proj_kernel,
        grid=(t // tm,),
        in_specs=[pl.BlockSpec((tm, d), lambda i: (i, 0)), _resident((1, d)), _resident((d, n)),
                  _resident((nt, d))],
        out_specs=[pl.BlockSpec((tm, n), lambda i: (i, 0)), pl.BlockSpec((nt, tm), lambda i: (0, i))],
        out_shape=[jax.ShapeDtypeStruct((t, n), BF16), jax.ShapeDtypeStruct((nt, t), BF16)],
        compiler_params=_params("parallel"),
        name="norm_proj",
    )(x, g, w, wt)


def _out_proj_kernel(*refs, n_act):
    acts = refs[:n_act]
    w_ref, g_ref, x_ref, o_ref = refs[n_act:]
    k0 = 0
    m = None
    for a in acts:
        k = a.shape[1]
        part = jnp.dot(a[...], w_ref[k0:k0 + k, :], preferred_element_type=F32)
        m = part if m is None else m + part
        k0 += k
    o_ref[...] = x_ref[...] + _rms_rows(m, g_ref[...])


def _out_proj(acts, w, g, x, tm=512):
    t, d = x.shape
    in_specs = [pl.BlockSpec((tm, a.shape[1]), lambda i: (i, 0)) for a in acts]
    in_specs += [_resident(w.shape), _resident((1, d)), pl.BlockSpec((tm, d), lambda i: (i, 0))]
    return pl.pallas_call(
        functools.partial(_out_proj_kernel, n_act=len(acts)),
        grid=(t // tm,),
        in_specs=in_specs,
        out_specs=pl.BlockSpec((tm, d), lambda i: (i, 0)),
        out_shape=jax.ShapeDtypeStruct((t, d), F32),
        compiler_params=_params("parallel"),
        name="out_proj",
    )(*acts, w, g, x)


def _halo_specs(tm, d, t):
    nb = tm // HALO
    last = t // HALO - 1
    return [
        pl.BlockSpec((HALO, d), lambda i: (jnp.maximum(i * nb - 1, 0), 0)),
        pl.BlockSpec((tm, d), lambda i: (i, 0)),
        pl.BlockSpec((HALO, d), lambda i: (jnp.minimum((i + 1) * nb, last), 0)),
    ]


def _normed_tile_with_halo(xp_ref, x_ref, xn_ref, g_ref, tiles_per_seq):
    i = pl.program_id(0)
    pos = i % tiles_per_seq
    has_prev = (pos != 0).astype(F32)
    has_next = (pos != tiles_per_seq - 1).astype(F32)
    g = g_ref[...]
    top = _rms_rows(xp_ref[...], g) * has_prev
    mid = _rms_rows(x_ref[...], g)
    bot = _rms_rows(xn_ref[...], g) * has_next
    return jnp.concatenate([top, mid, bot], axis=0).astype(BF16)


def _depthwise_conv_rows(u, w_ref, b_ref, cols, tm):
    n = u.shape[0]
    width = w_ref.shape[0]
    pad = width // 2
    acc = None
    for k in range(width):
        shift = pad - k
        us = u if shift == 0 else pltpu.roll(u, shift % n, 0)
        term = us[HALO:HALO + tm] * w_ref[k:k + 1, cols]
        acc = term if acc is None else acc + term
    return acc + b_ref[:, cols]


def _ffn_kernel(xp_ref, x_ref, xn_ref, gpre_ref, wup_ref, cw_ref, cb_ref, wdn_ref, gpost_ref,
                o_ref, *, tiles_per_seq):
    tm = x_ref.shape[0]
    f = wdn_ref.shape[0]
    hn = _normed_tile_with_halo(xp_ref, x_ref, xn_ref, gpre_ref, tiles_per_seq)
    gcols = slice(0, f)
    vcols = slice(f, 2 * f)
    ug = jnp.dot(hn, wup_ref[:, gcols], preferred_element_type=F32)
    uv = jnp.dot(hn, wup_ref[:, vcols], preferred_element_type=F32)
    gate = _depthwise_conv_rows(ug, cw_ref, cb_ref, gcols, tm)
    val = _depthwise_conv_rows(uv, cw_ref, cb_ref, vcols, tm)
    h = (jax.nn.gelu(gate, approximate=True) * val).astype(BF16)
    m = jnp.dot(h, wdn_ref[...], preferred_element_type=F32)
    o_ref[...] = x_ref[...] + _rms_rows(m, gpost_ref[...])


def _ffn(x, gpre, wup, cw, cb, wdn, gpost, seq, tm=512):
    t, d = x.shape
    f = wdn.shape[0]
    return pl.pallas_call(
        functools.partial(_ffn_kernel, tiles_per_seq=seq // tm),
        grid=(t // tm,),
        in_specs=_halo_specs(tm, d, t) + [
            _resident((1, d)), _resident(wup.shape), _resident(cw.shape), _resident((1, 2 * f)),
            _resident(wdn.shape), _resident((1, d))],
        out_specs=pl.BlockSpec((tm, d), lambda i: (i, 0)),
        out_shape=jax.ShapeDtypeStruct((t, d), F32),
        compiler_params=_params("parallel"),
        name="ffn",
    )(x, x, x, gpre, wup, cw, cb, wdn, gpost)


def _ssd_inproj_kernel(xp_ref, x_ref, xn_ref, g_ref, wz_ref, wx_ref, wdt_ref, cw_ref, cb_ref,
                       z_ref, xbc_ref, dt_ref, *, tiles_per_seq, ncol):
    tm = x_ref.shape[0]
    hn = _normed_tile_with_halo(xp_ref, x_ref, xn_ref, g_ref, tiles_per_seq)
    hm = hn[HALO:HALO + tm]
    z_ref[...] = jnp.dot(hm, wz_ref[...], preferred_element_type=F32).astype(z_ref.dtype)
    dt_ref[...] = jnp.dot(hm, wdt_ref[...], preferred_element_type=F32)
    for c in range(0, wx_ref.shape[1], ncol):
        cols = slice(c, c + ncol)
        u = jnp.dot(hn, wx_ref[:, cols], preferred_element_type=F32)
        v = _depthwise_conv_rows(u, cw_ref, cb_ref, cols, tm)
        xbc_ref[:, cols] = (v * jax.nn.sigmoid(v)).astype(xbc_ref.dtype)


def _ssd_inproj(x, g, wz, wx, wdt, cw, cb, seq, tm=512, ncol=SSD_INPROJ_CONV_COLS):
    t, d = x.shape
    nz, nx, ndt = wz.shape[1], wx.shape[1], wdt.shape[1]
    row = lambda n: pl.BlockSpec((tm, n), lambda i: (i, 0))
    return pl.pallas_call(
        functools.partial(_ssd_inproj_kernel, tiles_per_seq=seq // tm, ncol=ncol),
        grid=(t // tm,),
        in_specs=_halo_specs(tm, d, t) + [
            _resident((1, d)), _resident(wz.shape), _resident(wx.shape), _resident(wdt.shape),
            _resident(cw.shape), _resident((1, nx))],
        out_specs=[row(nz), row(nx), row(ndt)],
        out_shape=[jax.ShapeDtypeStruct((t, nz), BF16), jax.ShapeDtypeStruct((t, nx), BF16),
                   jax.ShapeDtypeStruct((t, ndt), F32)],
        compiler_params=_params("parallel"),
        name="ssd_inproj",
    )(x, x, x, g, wz, wx, wdt, cw, cb)


def _chunk_of(p, c, nc):
    return p * c + (1 - p) * (nc - 1 - c)


def _first_half_mask(shape):
    return lax.broadcasted_iota(jnp.int32, shape, len(shape) - 1) < HEAD_DIM


def _ret_kernel(q_ref, k_ref, v_ref, g_ref, cos_ref, sin_ref, lgf_ref, lgb_ref, gn_ref, gm_ref,
                o_ref, hf, hb, hbs, dm, *, nc):
    p = pl.program_id(1)
    c = pl.program_id(2)
    L = k_ref.shape[0]
    npair = k_ref.shape[1] // LANES
    lgf = lgf_ref[...]
    lgb = lgb_ref[...]
    first = _first_half_mask((1, LANES))
    swap_up = (lax.broadcasted_iota(jnp.int32, (1, LANES), 1) % HEAD_DIM) < HEAD_DIM // 2
    ri = lax.broadcasted_iota(jnp.int32, (LANES, LANES), 0) < HEAD_DIM
    ci = lax.broadcasted_iota(jnp.int32, (LANES, LANES), 1) < HEAD_DIM
    same_head = ri == ci
    jj = lax.broadcasted_iota(jnp.int32, (L, 1), 0).astype(F32)
    cos_f = cos_ref[...]
    sin_s = sin_ref[...]
    scale = HEAD_DIM ** -0.5

    def rot(x):
        sw = jnp.where(swap_up, pltpu.roll(x, LANES - HEAD_DIM // 2, 1),
                       pltpu.roll(x, HEAD_DIM // 2, 1))
        return x * cos_f + sw * sin_s

    @pl.when((p == 0) & (c == 0))
    def _init():
        hf[...] = jnp.zeros_like(hf)
        hb[...] = jnp.zeros_like(hb)
        d = (lax.broadcasted_iota(jnp.int32, (L, L), 0)
             - lax.broadcasted_iota(jnp.int32, (L, L), 1)).astype(F32)
        for h in range(2 * npair):
            lf = lgf[:, HEAD_DIM * h:HEAD_DIM * h + 1]
            lb = lgb[:, HEAD_DIM * h:HEAD_DIM * h + 1]
            dm[h] = jnp.exp(jnp.where(d >= 0, d * lf, -d * lb))

    @pl.when(p == 0)
    def _backward_states():
        ch = nc - 1 - c
        k = k_ref[...].astype(F32)
        tail_b = jnp.exp(jj * lgb)
        decay_b = jnp.exp(L * lgb)
        for pr in range(npair):
            sl = slice(LANES * pr, LANES * (pr + 1))
            kp = rot(k[:, sl]) * scale
            kt = (kp * tail_b[:, sl]).T.astype(BF16)
            hbs[ch, pr] = hb[pr]
            upd = jnp.dot(kt, v_ref[:, sl], preferred_element_type=F32)
            hb[pr] = jnp.where(same_head, hb[pr] * decay_b[:, sl] + upd, 0.0)

    @pl.when(p == 1)
    def _forward():
        q = q_ref[...].astype(F32)
        k = k_ref[...].astype(F32)
        dec_f = jnp.exp((jj + 1.0) * lgf)
        dec_b = jnp.exp((L - jj) * lgb)
        tail_f = jnp.exp((L - 1.0 - jj) * lgf)
        decay_f = jnp.exp(L * lgf)
        ys = []
        for pr in range(npair):
            sl = slice(LANES * pr, LANES * (pr + 1))
            qp = rot(q[:, sl])
            kp = rot(k[:, sl]) * scale
            kb = kp.astype(BF16)
            vp = v_ref[:, sl]
            res = []
            for hh in range(2):
                keep = first if hh == 0 else jnp.logical_not(first)
                qm = jnp.where(keep, qp, 0.0).astype(BF16)
                s = lax.dot_general(qm, kb, (((1,), (1,)), ((), ())), preferred_element_type=F32)
                a = (s * dm[2 * pr + hh]).astype(BF16)
                res.append(jnp.dot(a, vp, preferred_element_type=F32))
            y = jnp.where(first, res[0], res[1])
            y = y + jnp.dot((qp * dec_f[:, sl]).astype(BF16), hf[pr].astype(BF16),
                            preferred_element_type=F32)
            y = y + jnp.dot((qp * dec_b[:, sl]).astype(BF16), hbs[c, pr].astype(BF16),
                            preferred_element_type=F32)
            kt = (kp * tail_f[:, sl]).T.astype(BF16)
            upd = jnp.dot(kt, vp, preferred_element_type=F32)
            hf[pr] = jnp.where(same_head, hf[pr] * decay_f[:, sl] + upd, 0.0)
            ys.append(y)
        y = jnp.concatenate(ys, axis=1)
        gm = gm_ref[...]
        mu = _split2_dot(y, gm)
        d = y - mu
        var = _split2_dot(d * d, gm)
        yn = d * lax.rsqrt(var + EPS) * gn_ref[...]
        g = g_ref[...].astype(F32)
        o_ref[...] = (g * jax.nn.sigmoid(g) * yn).astype(o_ref.dtype)


def _retention(proj, cos_t, sin_t, lgf, lgb, gn, gmat, batch, seq, width):
    L = SCAN_CHUNK
    nc = seq // L
    nw = width // LANES
    t = proj.shape[0]

    def kv_map(col):
        return lambda b, p, c: (b * nc + _chunk_of(p, c, nc), col)

    def fwd_map(col):
        return lambda b, p, c: (b * nc + p * c, col)

    tab_map = lambda b, p, c: (_chunk_of(p, c, nc), 0)
    return pl.pallas_call(
        functools.partial(_ret_kernel, nc=nc),
        grid=(batch, 2, nc),
        in_specs=[pl.BlockSpec((L, width), fwd_map(0)), pl.BlockSpec((L, width), kv_map(1)),
                  pl.BlockSpec((L, width), kv_map(2)), pl.BlockSpec((L, width), fwd_map(3)),
                  pl.BlockSpec((L, LANES), tab_map), pl.BlockSpec((L, LANES), tab_map),
                  _resident((1, width)), _resident((1, width)), _resident((1, width)),
                  _resident((width, width))],
        out_specs=pl.BlockSpec((L, width), fwd_map(0)),
        out_shape=jax.ShapeDtypeStruct((t, width), BF16),
        scratch_shapes=[pltpu.VMEM((nw, LANES, LANES), F32), pltpu.VMEM((nw, LANES, LANES), F32),
                        pltpu.VMEM((nc, nw, LANES, LANES), F32), pltpu.VMEM((2 * nw, L, L), F32)],
        compiler_params=_params("arbitrary", "arbitrary", "arbitrary"),
        name="retention",
    )(proj, proj, proj, proj, cos_t, sin_t, lgf, lgb, gn, gmat)


def _na_kernel(q_ref, kp_ref, kc_ref, kn_ref, vp_ref, vc_ref, vn_ref, bias_ref, o_ref,
               kbuf, vtb, *, rows_total):
    i = pl.program_id(1)
    halo = kp_ref.shape[0]
    main = kc_ref.shape[0]
    span = main + 2 * halo
    nkeys = NA_WIN_R * GRID_W
    npair = o_ref.shape[1] // LANES
    kbuf[0:halo] = kp_ref[...]
    kbuf[halo:halo + main] = kc_ref[...]
    kbuf[halo + main:] = kn_ref[...]
    vtb[0, :, 0:halo] = vp_ref[...]
    vtb[0, :, halo:halo + main] = vc_ref[...]
    vtb[0, :, halo + main:] = vn_ref[...]
    vtb[1] = pltpu.roll(vtb[0], span - GRID_W, 1)
    first = _first_half_mask((1, LANES))
    scale = HEAD_DIM ** -0.5
    buf_row0 = i * NA_ROWS_PER_STEP - NA_WIN_R // 2

    def row_body(r, carry):
        gr = i * NA_ROWS_PER_STEP + r
        r0 = jnp.clip(gr - NA_WIN_R // 2, 0, rows_total - NA_WIN_R)
        delta = gr - r0
        rel = r0 - buf_row0
        odd = rel % 2
        off = pl.multiple_of(rel * GRID_W, GRID_W)
        base = pl.multiple_of((rel - odd) * GRID_W, LANES)
        qoff = pl.multiple_of(r * GRID_W, GRID_W)
        qrow = q_ref[pl.ds(qoff, GRID_W), :].astype(F32) * scale
        outs = []
        for pr in range(npair):
            sl = slice(LANES * pr, LANES * (pr + 1))
            qp = qrow[:, sl]
            qm = jnp.concatenate([jnp.where(first, qp, 0.0), jnp.where(first, 0.0, qp)],
                                 axis=0).astype(BF16)
            kwin = kbuf[pl.ds(off, nkeys), sl]
            st = lax.dot_general(kwin, qm, (((1,), (1,)), ((), ())), preferred_element_type=F32)
            st = st + bias_ref[pr, delta]
            m = jnp.max(st, axis=0, keepdims=True)
            e = jnp.exp(st - m)
            l = jnp.sum(e, axis=0, keepdims=True)
            vtw = vtb[odd, sl, pl.ds(base, nkeys)]
            ot = jnp.dot(vtw, e.astype(BF16), preferred_element_type=F32) / l
            o = ot.T
            outs.append(jnp.where(first, o[0:GRID_W], o[GRID_W:2 * GRID_W]))
        o_ref[pl.ds(qoff, GRID_W), :] = jnp.concatenate(outs, axis=1).astype(o_ref.dtype)
        return carry

    lax.fori_loop(0, NA_ROWS_PER_STEP, row_body, 0, unroll=NA_ROW_UNROLL)


def _neighborhood_attention(proj, vt, bias, batch, seq, width, q_col, k_col):
    assert 2 * GRID_W == LANES
    rows = seq // GRID_W
    nstep = rows // NA_ROWS_PER_STEP
    main = NA_ROWS_PER_STEP * GRID_W
    halo = (NA_WIN_R // 2) * GRID_W
    per = main // halo
    nhalo = seq // halo
    t = proj.shape[0]

    main_i = lambda b, i: b * nstep + i
    prev_i = lambda b, i: b * nhalo + jnp.maximum(i * per - 1, 0)
    next_i = lambda b, i: b * nhalo + jnp.minimum((i + 1) * per, nhalo - 1)
    rowblk = lambda f, col: (lambda b, i: (f(b, i), col))
    colblk = lambda f: (lambda b, i: (0, f(b, i)))

    return pl.pallas_call(
        functools.partial(_na_kernel, rows_total=rows),
        grid=(batch, nstep),
        in_specs=[pl.BlockSpec((main, width), rowblk(main_i, q_col)),
                  pl.BlockSpec((halo, width), rowblk(prev_i, k_col)),
                  pl.BlockSpec((main, width), rowblk(main_i, k_col)),
                  pl.BlockSpec((halo, width), rowblk(next_i, k_col)),
                  pl.BlockSpec((width, halo), colblk(prev_i)),
                  pl.BlockSpec((width, main), colblk(main_i)),
                  pl.BlockSpec((width, halo), colblk(next_i)),
                  _resident(bias.shape)],
        out_specs=pl.BlockSpec((main, width), rowblk(main_i, 0)),
        out_shape=jax.ShapeDtypeStruct((t, width), BF16),
        scratch_shapes=[pltpu.VMEM((main + 2 * halo, width), BF16),
                        pltpu.VMEM((2, width, main + 2 * halo), BF16)],
        compiler_params=_params("parallel", "parallel"),
        name="neighborhood_attention",
    )(proj, proj, proj, proj, vt, vt, vt, bias)


def _na_bias_table(rpb):
    qc = jnp.arange(GRID_W)
    kc = jnp.arange(GRID_W)
    cstart = jnp.clip(qc - NA_WIN_C // 2, 0, GRID_W - NA_WIN_C)
    valid = (kc[None, :] >= cstart[:, None]) & (kc[None, :] < cstart[:, None] + NA_WIN_C)
    dc = jnp.clip(kc[None, :] - qc[:, None], -(NA_WIN_C - 1), NA_WIN_C - 1) + (NA_WIN_C - 1)
    onehot = (dc[None] == jnp.arange(2 * NA_WIN_C - 1)[:, None, None]).astype(F32)
    toep = jnp.einsum('hrd,dqk->hrqk', rpb.astype(F32), onehot, precision=lax.Precision.HIGHEST)
    toep = jnp.where(valid[None, None], toep, NEG_BIG)
    per_delta = [toep[:, NA_WIN_R - 1 - dl:2 * NA_WIN_R - 1 - dl] for dl in range(NA_WIN_R)]
    tab = jnp.stack(per_delta, axis=1)
    nh = rpb.shape[0]
    tab = tab.reshape(nh // 2, 2, NA_WIN_R, NA_WIN_R, GRID_W, GRID_W)
    tab = jnp.transpose(tab, (0, 2, 3, 5, 1, 4))
    return tab.reshape(nh // 2, NA_WIN_R, NA_WIN_R * GRID_W, 2 * GRID_W)


def _ssd_kernel(x_ref, b_ref, c_ref, dt_ref, z_ref, dtb_ref, alog_ref, dsk_ref, ng_ref,
                o_ref, hf, hb, hbs, *, nc, hpg, groups):
    p = pl.program_id(1)
    c = pl.program_id(2)
    L = x_ref.shape[0]
    assert L == SSD_STATE == LANES
    gw = hpg * SSD_HEADDIM
    npair = gw // LANES
    first = _first_half_mask((1, LANES))
    rj = lax.broadcasted_iota(jnp.int32, (L, L), 0)
    rl = lax.broadcasted_iota(jnp.int32, (L, L), 1)
    lower = rj >= rl
    tri = lower.astype(BF16)

    @pl.when((p == 0) & (c == 0))
    def _init():
        hf[...] = jnp.zeros_like(hf)
        hb[...] = jnp.zeros_like(hb)

    raw = dt_ref[...] + dtb_ref[...]
    dt = jnp.maximum(raw, 0.0) + jnp.log1p(jnp.exp(-jnp.abs(raw)))
    a = dt * (-jnp.exp(alog_ref[...]))
    cs = _split3_dot(tri, a)
    cse = cs - a
    cs_t = cs.T
    cse_t = cse.T
    dt_t = dt.T
    tot = cs[L - 1:L, :]

    def pair_scale(vals):
        return jnp.where(first, vals[0], vals[1])

    def row(m, lane):
        return m[lane:lane + 1, :]

    def col_bcast(m, lane):
        return jnp.broadcast_to(m[:, lane:lane + 1], (L, LANES))

    @pl.when(p == 0)
    def _backward_states():
        ch = nc - 1 - c
        for g in range(groups):
            bt = b_ref[:, SSD_STATE * g:SSD_STATE * (g + 1)].astype(F32).T
            for pr in range(npair):
                idx = g * npair + pr
                xp = x_ref[:, LANES * idx:LANES * (idx + 1)]
                upd, dec = [], []
                for hh in range(2):
                    lb = 2 * hpg * g + hpg + 2 * pr + hh
                    srow = jnp.exp(row(cse_t, lb)) * row(dt_t, lb)
                    upd.append(jnp.dot((bt * srow).astype(BF16), xp, preferred_element_type=F32))
                    dec.append(jnp.exp(tot[:, lb:lb + 1]))
                hbs[ch, idx] = hb[idx].astype(BF16)
                hb[idx] = hb[idx] * pair_scale(dec) + pair_scale(upd)

    @pl.when(p == 1)
    def _forward():
        for g in range(groups):
            bm = b_ref[:, SSD_STATE * g:SSD_STATE * (g + 1)]
            cm = c_ref[:, SSD_STATE * g:SSD_STATE * (g + 1)]
            bt = bm.astype(F32).T
            cb = lax.dot_general(cm, bm, (((1,), (1,)), ((), ())), preferred_element_type=F32)
            cf32 = cm.astype(F32)
            ys = []
            for pr in range(npair):
                idx = g * npair + pr
                sl = slice(LANES * idx, LANES * (idx + 1))
                xp = x_ref[:, sl]
                hf_b = hf[idx].astype(BF16)
                hb_b = hbs[c, idx]
                res, upd, dec = [], [], []
                for hh in range(2):
                    lf = 2 * hpg * g + 2 * pr + hh
                    lb = lf + hpg
                    cfc = col_bcast(cs, lf)
                    cbc = col_bcast(cse, lb)
                    cfr = row(cs_t, lf)
                    e = jnp.where(lower, cfc - cfr, row(cse_t, lb) - cbc)
                    w = jnp.exp(e) * jnp.where(lower, row(dt_t, lf), row(dt_t, lb))
                    m = (cb * w).astype(BF16)
                    c_f = (cf32 * jnp.exp(cfc)).astype(BF16)
                    c_b = (cf32 * jnp.exp(tot[:, lb:lb + 1] - cbc)).astype(BF16)
                    res.append(jnp.dot(m, xp, preferred_element_type=F32)
                               + jnp.dot(c_f, hf_b, preferred_element_type=F32)
                               + jnp.dot(c_b, hb_b, preferred_element_type=F32))
                    srow = jnp.exp(tot[:, lf:lf + 1] - cfr) * row(dt_t, lf)
                    upd.append(jnp.dot((bt * srow).astype(BF16), xp, preferred_element_type=F32))
                    dec.append(jnp.exp(tot[:, lf:lf + 1]))
                hf[idx] = hf[idx] * pair_scale(dec) + pair_scale(upd)
                y = pair_scale(res) + xp.astype(F32) * dsk_ref[:, sl]
                z = z_ref[:, sl].astype(F32)
                ys.append(y * (z * jax.nn.sigmoid(z)))
            y = jnp.concatenate(ys, axis=1)
            gsl = slice(gw * g, gw * (g + 1))
            o_ref[:, gsl] = _rms_rows(y, ng_ref[:, gsl]).astype(o_ref.dtype)


def _ssd_scan(xbc, dt, z, dtb, alog, dsk, ng, batch, seq, inner, hpg):
    L = SCAN_CHUNK
    nc = seq // L
    groups = inner // (hpg * SSD_HEADDIM)
    npairs = inner // LANES
    bc_w = groups * SSD_STATE
    t = xbc.shape[0]
    b_col = inner // bc_w

    def any_map(col):
        return lambda b, p, c: (b * nc + _chunk_of(p, c, nc), col)

    def fwd_map(col):
        return lambda b, p, c: (b * nc + p * c, col)

    return pl.pallas_call(
        functools.partial(_ssd_kernel, nc=nc, hpg=hpg, groups=groups),
        grid=(batch, 2, nc),
        in_specs=[pl.BlockSpec((L, inner), any_map(0)),
                  pl.BlockSpec((L, bc_w), any_map(b_col)),
                  pl.BlockSpec((L, bc_w), fwd_map(b_col + 1)),
                  pl.BlockSpec((L, LANES), any_map(0)),
                  pl.BlockSpec((L, inner), fwd_map(0)),
                  _resident((1, LANES)), _resident((1, LANES)),
                  _resident((1, inner)), _resident((1, inner))],
        out_specs=pl.BlockSpec((L, inner), fwd_map(0)),
        out_shape=jax.ShapeDtypeStruct((t, inner), BF16),
        scratch_shapes=[pltpu.VMEM((npairs, SSD_STATE, LANES), F32),
                        pltpu.VMEM((npairs, SSD_STATE, LANES), F32),
                        pltpu.VMEM((nc, npairs, SSD_STATE, LANES), BF16)],
        compiler_params=_params("arbitrary", "arbitrary", "arbitrary"),
        name="ssd_scan",
    )(xbc, xbc, xbc, dt, z, dtb, alog, dsk, ng)


def _rotary_tables(seq):
    half = HEAD_DIM // 2
    inv = 1.0 / (ROPE_BASE ** (jnp.arange(half, dtype=F32) / half))
    ang = jnp.arange(seq).astype(F32)[:, None] * inv[None, :]
    cos = jnp.cos(ang)
    sin = jnp.sin(ang)
    reps = LANES // HEAD_DIM
    cos_t = jnp.tile(jnp.concatenate([cos, cos], axis=1), (1, reps))
    sin_t = jnp.tile(jnp.concatenate([-sin, sin], axis=1), (1, reps))
    return cos_t, sin_t


def _group_dt_lanes(v, hpg):
    lead = v.shape[:-2]
    heads = v.shape[-1]
    groups = heads // hpg
    assert 2 * heads <= LANES
    vg = v.reshape(*lead, 2, groups, hpg)
    vg = jnp.moveaxis(vg, -3, -2).reshape(*lead, 2 * heads)
    pad = [(0, 0)] * (vg.ndim - 1) + [(0, LANES - 2 * heads)]
    return jnp.pad(vg, pad)


def kernel(x, norm_mix_pre, norm_mix_post, norm_ffn_pre, norm_ffn_post, ab_w_in, ab_ret_decay_logit, ab_ret_gn_g, ab_na_rpb, ab_w_out, c_w_in, c_conv_w, c_conv_b, c_dt_bias, c_a_log, c_d_skip, c_norm_g, c_w_out, ffn_w_up, ffn_conv_w, ffn_conv_b, ffn_w_down):
    batch, seq, d = x.shape
    depth = norm_mix_pre.shape[0]
    t = batch * seq
    ret_w = RET_HEADS * HEAD_DIM
    na_w = NA_HEADS * HEAD_DIM
    inner = c_w_out.shape[1]
    heads = inner // SSD_HEADDIM
    hpg = heads // SSD_GROUPS
    xbc_w = inner + 2 * SSD_GROUPS * SSD_STATE
    assert ret_w == na_w and ab_w_in.shape[2] == 4 * ret_w + 3 * na_w

    h = x.reshape(t, d).astype(F32)
    cos_t, sin_t = _rotary_tables(seq)
    gmat = jnp.kron(jnp.eye(RET_HEADS, dtype=F32),
                    jnp.full((HEAD_DIM, HEAD_DIM), 1.0 / HEAD_DIM, F32)).astype(BF16)
    row = lambda v: v.reshape(1, -1).astype(F32)

    for layer in range(depth):
        i = layer // 2
        if layer % 2 == 0:
            n_tok = 4 * ret_w + 2 * na_w
            proj, na_vt = _norm_proj(h, row(norm_mix_pre[layer]),
                                     ab_w_in[i][:, :n_tok].astype(BF16),
                                     ab_w_in[i][:, n_tok:].T.astype(BF16))
            lg = -jax.nn.softplus(-ab_ret_decay_logit[i].astype(F32))
            lgf = jnp.repeat(lg[0], HEAD_DIM).reshape(1, ret_w)
            lgb = jnp.repeat(lg[1], HEAD_DIM).reshape(1, ret_w)
            ret = _retention(proj, cos_t, sin_t, lgf, lgb, row(ab_ret_gn_g[i]), gmat,
                             batch, seq, ret_w)
            na = _neighborhood_attention(proj, na_vt, _na_bias_table(ab_na_rpb[i]), batch, seq, na_w,
                                         4 * ret_w // na_w, 4 * ret_w // na_w + 1)
            h = _out_proj([ret, na], ab_w_out[i].astype(BF16), row(norm_mix_post[layer]), h)
        else:
            w_in = c_w_in[i]
            wz = w_in[:, :inner].astype(BF16)
            wx = w_in[:, inner:inner + xbc_w].astype(BF16)
            wdt = _group_dt_lanes(w_in[:, inner + xbc_w:].reshape(d, 2, heads), hpg).astype(BF16)
            z, xbc, dtr = _ssd_inproj(h, row(norm_mix_pre[layer]), wz, wx, wdt,
                                      c_conv_w[i].astype(F32), row(c_conv_b[i]), seq)
            dtb = _group_dt_lanes(c_dt_bias[i].astype(F32), hpg).reshape(1, -1)
            alog = _group_dt_lanes(c_a_log[i].astype(F32), hpg).reshape(1, -1)
            dsk = jnp.repeat(c_d_skip[i].astype(F32), SSD_HEADDIM).reshape(1, inner)
            y = _ssd_scan(xbc, dtr, z, dtb, alog, dsk, row(c_norm_g[i]), batch, seq, inner, hpg)
            h = _out_proj([y], c_w_out[i].astype(BF16), row(norm_mix_post[layer]), h)
        h = _ffn(h, row(norm_ffn_pre[layer]), ffn_w_up[layer].astype(BF16),
                 ffn_conv_w[layer].astype(F32), row(ffn_conv_b[layer]),
                 ffn_w_down[layer].astype(BF16), row(norm_ffn_post[layer]), seq)
    return h.reshape(batch, seq, d).astype(x.dtype)
```

```python
import functools
import math

import jax
import jax.numpy as jnp
from jax import lax
from jax.experimental import pallas as pl
from jax.experimental.pallas import tpu as pltpu

F32 = jnp.float32
BF16 = jnp.bfloat16

EPS = 1e-6
GRID_W = 64
ROPE_BASE = 10000.0
RET_HEADS = 8
NA_HEADS = 8
NA_WIN_R = 8
NA_WIN_C = 16
SSD_HEADDIM = 64
SSD_GROUPS = 4
SSD_STATE = 128
HEAD_DIM = 64
LANES = 128
HALO = 8
RET_CHUNK = 256
SSD_CHUNK = 128
NA_ROWS_PER_STEP = 32
NA_ROW_UNROLL = 8
SSD_INPROJ_CONV_COLS = 1024
NEG_BIG = -1e30
LOG2_E = 1.0 / math.log(2.0)
VMEM_LIMIT = 56 * 1024 * 1024


def _params(*sem):
    return pltpu.CompilerParams(dimension_semantics=sem, vmem_limit_bytes=VMEM_LIMIT)


def _resident(shape):
    nd = len(shape)
    return pl.BlockSpec(shape, lambda *_: (0,) * nd, pipeline_mode=pl.Buffered(1))


def _rms_rows(x, g):
    return x * lax.rsqrt(jnp.mean(x * x, axis=-1, keepdims=True) + EPS) * g


def _split3_dot(tri, a):
    hi = a.astype(BF16)
    r1 = a - hi.astype(F32)
    mid = r1.astype(BF16)
    lo = (r1 - mid.astype(F32)).astype(BF16)
    return (jnp.dot(tri, hi, preferred_element_type=F32)
            + jnp.dot(tri, mid, preferred_element_type=F32)
            + jnp.dot(tri, lo, preferred_element_type=F32))


def _split2_dot(a, m):
    hi = a.astype(BF16)
    lo = (a - hi.astype(F32)).astype(BF16)
    return jnp.dot(hi, m, preferred_element_type=F32) + jnp.dot(lo, m, preferred_element_type=F32)


def _norm_proj_kernel(x_ref, g_ref, w_ref, wt_ref, o_ref, ot_ref):
    hn = _rms_rows(x_ref[...], g_ref[...]).astype(BF16)
    o_ref[...] = jnp.dot(hn, w_ref[...], preferred_element_type=F32).astype(o_ref.dtype)
    ot_ref[...] = lax.dot_general(wt_ref[...], hn, (((1,), (1,)), ((), ())),
                                  preferred_element_type=F32).astype(ot_ref.dtype)


def _norm_proj(x, g, w, wt, tm=512):
    t, d = x.shape
    n = w.shape[1]
    nt = wt.shape[0]
    return pl.pallas_call(
        _norm_proj_kernel,
        grid=(t // tm,),
        in_specs=[pl.BlockSpec((tm, d), lambda i: (i, 0)), _resident((1, d)), _resident((d, n)),
                  _resident((nt, d))],
        out_specs=[pl.BlockSpec((tm, n), lambda i: (i, 0)), pl.BlockSpec((nt, tm), lambda i: (0, i))],
        out_shape=[jax.ShapeDtypeStruct((t, n), BF16), jax.ShapeDtypeStruct((nt, t), BF16)],
        compiler_params=_params("parallel"),
        name="norm_proj",
    )(x, g, w, wt)


def _out_proj_kernel(*refs, n_act):
    acts = refs[:n_act]
    w_ref, g_ref, x_ref, o_ref = refs[n_act:]
    k0 = 0
    m = None
    for a in acts:
        k = a.shape[1]
        part = jnp.dot(a[...], w_ref[k0:k0 + k, :], preferred_element_type=F32)
        m = part if m is None else m + part
        k0 += k
    o_ref[...] = x_ref[...] + _rms_rows(m, g_ref[...])


def _out_proj(acts, w, g, x, tm=512):
    t, d = x.shape
    in_specs = [pl.BlockSpec((tm, a.shape[1]), lambda i: (i, 0)) for a in acts]
    in_specs += [_resident(w.shape), _resident((1, d)), pl.BlockSpec((tm, d), lambda i: (i, 0))]
    return pl.pallas_call(
        functools.partial(_out_proj_kernel, n_act=len(acts)),
        grid=(t // tm,),
        in_specs=in_specs,
        out_specs=pl.BlockSpec((tm, d), lambda i: (i, 0)),
        out_shape=jax.ShapeDtypeStruct((t, d), F32),
        compiler_params=_params("parallel"),
        name="out_proj",
    )(*acts, w, g, x)


def _halo_specs(tm, d, t):
    nb = tm // HALO
    last = t // HALO - 1
    return [
        pl.BlockSpec((HALO, d), lambda i: (jnp.maximum(i * nb - 1, 0), 0)),
        pl.BlockSpec((tm, d), lambda i: (i, 0)),
        pl.BlockSpec((HALO, d), lambda i: (jnp.minimum((i + 1) * nb, last), 0)),
    ]


def _normed_tile_with_halo(xp_ref, x_ref, xn_ref, g_ref, tiles_per_seq):
    i = pl.program_id(0)
    pos = i % tiles_per_seq
    has_prev = (pos != 0).astype(F32)
    has_next = (pos != tiles_per_seq - 1).astype(F32)
    g = g_ref[...]
    top = _rms_rows(xp_ref[...], g) * has_prev
    mid = _rms_rows(x_ref[...], g)
    bot = _rms_rows(xn_ref[...], g) * has_next
    return jnp.concatenate([top, mid, bot], axis=0).astype(BF16)


def _depthwise_conv_rows(u, w_ref, b_ref, cols, tm):
    n = u.shape[0]
    width = w_ref.shape[0]
    pad = width // 2
    acc = None
    for k in range(width):
        shift = pad - k
        us = u if shift == 0 else pltpu.roll(u, shift % n, 0)
        term = us[HALO:HALO + tm] * w_ref[k:k + 1, cols]
        acc = term if acc is None else acc + term
    return acc + b_ref[:, cols]


def _ffn_kernel(xp_ref, x_ref, xn_ref, gpre_ref, wup_ref, cw_ref, cb_ref, wdn_ref, gpost_ref,
                o_ref, *, tiles_per_seq):
    tm = x_ref.shape[0]
    f = wdn_ref.shape[0]
    hn = _normed_tile_with_halo(xp_ref, x_ref, xn_ref, gpre_ref, tiles_per_seq)
    gcols = slice(0, f)
    vcols = slice(f, 2 * f)
    ug = jnp.dot(hn, wup_ref[:, gcols], preferred_element_type=F32)
    uv = jnp.dot(hn, wup_ref[:, vcols], preferred_element_type=F32)
    gate = _depthwise_conv_rows(ug, cw_ref, cb_ref, gcols, tm)
    val = _depthwise_conv_rows(uv, cw_ref, cb_ref, vcols, tm)
    h = (jax.nn.gelu(gate, approximate=True) * val).astype(BF16)
    m = jnp.dot(h, wdn_ref[...], preferred_element_type=F32)
    o_ref[...] = x_ref[...] + _rms_rows(m, gpost_ref[...])


def _ffn(x, gpre, wup, cw, cb, wdn, gpost, seq, tm=512):
    t, d = x.shape
    f = wdn.shape[0]
    return pl.pallas_call(
        functools.partial(_ffn_kernel, tiles_per_seq=seq // tm),
        grid=(t // tm,),
        in_specs=_halo_specs(tm, d, t) + [
            _resident((1, d)), _resident(wup.shape), _resident(cw.shape), _resident((1, 2 * f)),
            _resident(wdn.shape), _resident((1, d))],
        out_specs=pl.BlockSpec((tm, d), lambda i: (i, 0)),
        out_shape=jax.ShapeDtypeStruct((t, d), F32),
        compiler_params=_params("parallel"),
        name="ffn",
    )(x, x, x, gpre, wup, cw, cb, wdn, gpost)


def _ssd_inproj_kernel(xp_ref, x_ref, xn_ref, g_ref, wz_ref, wx_ref, wdt_ref, cw_ref, cb_ref,
                       z_ref, xbc_ref, dt_ref, *, tiles_per_seq, ncol):
    tm = x_ref.shape[0]
    hn = _normed_tile_with_halo(xp_ref, x_ref, xn_ref, g_ref, tiles_per_seq)
    hm = hn[HALO:HALO + tm]
    z_ref[...] = jnp.dot(hm, wz_ref[...], preferred_element_type=F32).astype(z_ref.dtype)
    dt_ref[...] = jnp.dot(hm, wdt_ref[...], preferred_element_type=F32)
    for c in range(0, wx_ref.shape[1], ncol):
        cols = slice(c, c + ncol)
        u = jnp.dot(hn, wx_ref[:, cols], preferred_element_type=F32)
        v = _depthwise_conv_rows(u, cw_ref, cb_ref, cols, tm)
        xbc_ref[:, cols] = (v * jax.nn.sigmoid(v)).astype(xbc_ref.dtype)


def _ssd_inproj(x, g, wz, wx, wdt, cw, cb, seq, tm=512, ncol=SSD_INPROJ_CONV_COLS):
    t, d = x.shape
    nz, nx, ndt = wz.shape[1], wx.shape[1], wdt.shape[1]
    row = lambda n: pl.BlockSpec((tm, n), lambda i: (i, 0))
    return pl.pallas_call(
        functools.partial(_ssd_inproj_kernel, tiles_per_seq=seq // tm, ncol=ncol),
        grid=(t // tm,),
        in_specs=_halo_specs(tm, d, t) + [
            _resident((1, d)), _resident(wz.shape), _resident(wx.shape), _resident(wdt.shape),
            _resident(cw.shape), _resident((1, nx))],
        out_specs=[row(nz), row(nx), row(ndt)],
        out_shape=[jax.ShapeDtypeStruct((t, nz), BF16), jax.ShapeDtypeStruct((t, nx), BF16),
                   jax.ShapeDtypeStruct((t, ndt), F32)],
        compiler_params=_params("parallel"),
        name="ssd_inproj",
    )(x, x, x, g, wz, wx, wdt, cw, cb)


def _chunk_of(p, c, nc):
    return p * c + (1 - p) * (nc - 1 - c)


def _first_half_mask(shape):
    return lax.broadcasted_iota(jnp.int32, shape, len(shape) - 1) < HEAD_DIM


def _ret_kernel(q_ref, k_ref, v_ref, g_ref, cos_ref, sin_ref, lgf_ref, lgb_ref, gn_ref, gm_ref,
                o_ref, hf, hb, hbs, dm, ptab, *, nc):
    p = pl.program_id(1)
    c = pl.program_id(2)
    L = k_ref.shape[0]
    npair = k_ref.shape[1] // LANES
    lgf = lgf_ref[...]
    lgb = lgb_ref[...]
    first = _first_half_mask((1, LANES))
    swap_up = (lax.broadcasted_iota(jnp.int32, (1, LANES), 1) % HEAD_DIM) < HEAD_DIM // 2
    ri = lax.broadcasted_iota(jnp.int32, (LANES, LANES), 0) < HEAD_DIM
    ci = lax.broadcasted_iota(jnp.int32, (LANES, LANES), 1) < HEAD_DIM
    same_head = ri == ci
    cos_f = cos_ref[...]
    sin_s = sin_ref[...]
    scale = HEAD_DIM ** -0.5
    TAIL_B, DEC_F, DEC_B, TAIL_F = range(4)

    def rot(x):
        sw = jnp.where(swap_up, pltpu.roll(x, LANES - HEAD_DIM // 2, 1),
                       pltpu.roll(x, HEAD_DIM // 2, 1))
        return x * cos_f + sw * sin_s

    @pl.when((p == 0) & (c == 0))
    def _init():
        hf[...] = jnp.zeros_like(hf)
        hb[...] = jnp.zeros_like(hb)
        d = (lax.broadcasted_iota(jnp.int32, (L, L), 0)
             - lax.broadcasted_iota(jnp.int32, (L, L), 1)).astype(F32)
        for h in range(2 * npair):
            lf = lgf[:, HEAD_DIM * h:HEAD_DIM * h + 1]
            lb = lgb[:, HEAD_DIM * h:HEAD_DIM * h + 1]
            dm[h] = jnp.exp(jnp.where(d >= 0, d * lf, -d * lb))
        jj = lax.broadcasted_iota(jnp.int32, (L, 1), 0).astype(F32)
        ptab[TAIL_B] = jnp.exp(jj * lgb)
        ptab[DEC_F] = jnp.exp((jj + 1.0) * lgf)
        ptab[DEC_B] = jnp.exp((L - jj) * lgb)
        ptab[TAIL_F] = jnp.exp((L - 1.0 - jj) * lgf)

    @pl.when(p == 0)
    def _backward_states():
        ch = nc - 1 - c
        k = k_ref[...].astype(F32)
        decay_b = jnp.exp(L * lgb)
        for pr in range(npair):
            sl = slice(LANES * pr, LANES * (pr + 1))
            kp = rot(k[:, sl]) * scale
            kt = (kp * ptab[TAIL_B, :, sl]).T.astype(BF16)
            hbs[ch, pr] = hb[pr].astype(BF16)
            upd = jnp.dot(kt, v_ref[:, sl], preferred_element_type=F32)
            hb[pr] = jnp.where(same_head, hb[pr] * decay_b[:, sl] + upd, 0.0)

    @pl.when(p == 1)
    def _forward():
        q = q_ref[...].astype(F32)
        k = k_ref[...].astype(F32)
        decay_f = jnp.exp(L * lgf)
        ys = []
        for pr in range(npair):
            sl = slice(LANES * pr, LANES * (pr + 1))
            qp = rot(q[:, sl])
            kp = rot(k[:, sl]) * scale
            kb = kp.astype(BF16)
            vp = v_ref[:, sl]
            res = []
            for hh in range(2):
                keep = first if hh == 0 else jnp.logical_not(first)
                qm = jnp.where(keep, qp, 0.0).astype(BF16)
                s = lax.dot_general(qm, kb, (((1,), (1,)), ((), ())), preferred_element_type=F32)
                a = (s * dm[2 * pr + hh]).astype(BF16)
                res.append(jnp.dot(a, vp, preferred_element_type=F32))
            y = jnp.where(first, res[0], res[1])
            y = y + jnp.dot((qp * ptab[DEC_F, :, sl]).astype(BF16), hf[pr].astype(BF16),
                            preferred_element_type=F32)
            y = y + jnp.dot((qp * ptab[DEC_B, :, sl]).astype(BF16), hbs[c, pr],
                            preferred_element_type=F32)
            kt = (kp * ptab[TAIL_F, :, sl]).T.astype(BF16)
            upd = jnp.dot(kt, vp, preferred_element_type=F32)
            hf[pr] = jnp.where(same_head, hf[pr] * decay_f[:, sl] + upd, 0.0)
            ys.append(y)
        y = jnp.concatenate(ys, axis=1)
        gm = gm_ref[...]
        mu = _split2_dot(y, gm)
        d = y - mu
        var = _split2_dot(d * d, gm)
        yn = d * lax.rsqrt(var + EPS) * gn_ref[...]
        g = g_ref[...].astype(F32)
        o_ref[...] = (g * jax.nn.sigmoid(g) * yn).astype(o_ref.dtype)


def _retention(proj, cos_t, sin_t, lgf, lgb, gn, gmat, batch, seq, width):
    L = RET_CHUNK
    nc = seq // L
    nw = width // LANES
    t = proj.shape[0]

    def kv_map(col):
        return lambda b, p, c: (b * nc + _chunk_of(p, c, nc), col)

    def fwd_map(col):
        return lambda b, p, c: (b * nc + p * c, col)

    tab_map = lambda b, p, c: (_chunk_of(p, c, nc), 0)
    return pl.pallas_call(
        functools.partial(_ret_kernel, nc=nc),
        grid=(batch, 2, nc),
        in_specs=[pl.BlockSpec((L, width), fwd_map(0)), pl.BlockSpec((L, width), kv_map(1)),
                  pl.BlockSpec((L, width), kv_map(2)), pl.BlockSpec((L, width), fwd_map(3)),
                  pl.BlockSpec((L, LANES), tab_map), pl.BlockSpec((L, LANES), tab_map),
                  _resident((1, width)), _resident((1, width)), _resident((1, width)),
                  _resident((width, width))],
        out_specs=pl.BlockSpec((L, width), fwd_map(0)),
        out_shape=jax.ShapeDtypeStruct((t, width), BF16),
        scratch_shapes=[pltpu.VMEM((nw, LANES, LANES), F32), pltpu.VMEM((nw, LANES, LANES), F32),
                        pltpu.VMEM((nc, nw, LANES, LANES), BF16), pltpu.VMEM((2 * nw, L, L), F32),
                        pltpu.VMEM((4, L, width), F32)],
        compiler_params=_params("arbitrary", "arbitrary", "arbitrary"),
        name="retention",
    )(proj, proj, proj, proj, cos_t, sin_t, lgf, lgb, gn, gmat)


def _na_kernel(q_ref, kp_ref, kc_ref, kn_ref, vp_ref, vc_ref, vn_ref, bias_ref, o_ref,
               kbuf, vtb, *, rows_total):
    i = pl.program_id(1)
    halo = kp_ref.shape[0]
    main = kc_ref.shape[0]
    span = main + 2 * halo
    nkeys = NA_WIN_R * GRID_W
    npair = o_ref.shape[1] // LANES
    kbuf[0:halo] = kp_ref[...]
    kbuf[halo:halo + main] = kc_ref[...]
    kbuf[halo + main:] = kn_ref[...]
    vtb[0, :, 0:halo] = vp_ref[...]
    vtb[0, :, halo:halo + main] = vc_ref[...]
    vtb[0, :, halo + main:] = vn_ref[...]
    vtb[1] = pltpu.roll(vtb[0], span - GRID_W, 1)
    first = _first_half_mask((1, LANES))
    scale = HEAD_DIM ** -0.5
    buf_row0 = i * NA_ROWS_PER_STEP - NA_WIN_R // 2

    def row_body(r, carry):
        gr = i * NA_ROWS_PER_STEP + r
        r0 = jnp.clip(gr - NA_WIN_R // 2, 0, rows_total - NA_WIN_R)
        boff = pl.multiple_of((NA_WIN_R - 1 - (gr - r0)) * GRID_W, GRID_W)
        rel = r0 - buf_row0
        odd = rel % 2
        off = pl.multiple_of(rel * GRID_W, GRID_W)
        base = pl.multiple_of((rel - odd) * GRID_W, LANES)
        qoff = pl.multiple_of(r * GRID_W, GRID_W)
        qrow = q_ref[pl.ds(qoff, GRID_W), :].astype(F32) * scale
        outs = []
        for pr in range(npair):
            sl = slice(LANES * pr, LANES * (pr + 1))
            qp = qrow[:, sl]
            qm = jnp.concatenate([jnp.where(first, qp, 0.0), jnp.where(first, 0.0, qp)],
                                 axis=0).astype(BF16)
            kwin = kbuf[pl.ds(off, nkeys), sl]
            st = lax.dot_general(kwin, qm, (((1,), (1,)), ((), ())), preferred_element_type=F32)
            st = st + bias_ref[pr, pl.ds(boff, nkeys), :]
            m = jnp.max(st, axis=0, keepdims=True)
            e = jnp.exp(st - m)
            l = jnp.sum(e, axis=0, keepdims=True)
            vtw = vtb[odd, sl, pl.ds(base, nkeys)]
            ot = jnp.dot(vtw, e.astype(BF16), preferred_element_type=F32) / l
            o = ot.T
            outs.append(jnp.where(first, o[0:GRID_W], o[GRID_W:2 * GRID_W]))
        o_ref[pl.ds(qoff, GRID_W), :] = jnp.concatenate(outs, axis=1).astype(o_ref.dtype)
        return carry

    lax.fori_loop(0, NA_ROWS_PER_STEP, row_body, 0, unroll=NA_ROW_UNROLL)


def _neighborhood_attention(proj, vt, bias, batch, seq, width, q_col, k_col):
    assert 2 * GRID_W == LANES
    rows = seq // GRID_W
    nstep = rows // NA_ROWS_PER_STEP
    main = NA_ROWS_PER_STEP * GRID_W
    halo = (NA_WIN_R // 2) * GRID_W
    per = main // halo
    nhalo = seq // halo
    t = proj.shape[0]

    main_i = lambda b, i: b * nstep + i
    prev_i = lambda b, i: b * nhalo + jnp.maximum(i * per - 1, 0)
    next_i = lambda b, i: b * nhalo + jnp.minimum((i + 1) * per, nhalo - 1)
    rowblk = lambda f, col: (lambda b, i: (f(b, i), col))
    colblk = lambda f: (lambda b, i: (0, f(b, i)))

    return pl.pallas_call(
        functools.partial(_na_kernel, rows_total=rows),
        grid=(batch, nstep),
        in_specs=[pl.BlockSpec((main, width), rowblk(main_i, q_col)),
                  pl.BlockSpec((halo, width), rowblk(prev_i, k_col)),
                  pl.BlockSpec((main, width), rowblk(main_i, k_col)),
                  pl.BlockSpec((halo, width), rowblk(next_i, k_col)),
                  pl.BlockSpec((width, halo), colblk(prev_i)),
                  pl.BlockSpec((width, main), colblk(main_i)),
                  pl.BlockSpec((width, halo), colblk(next_i)),
                  _resident(bias.shape)],
        out_specs=pl.BlockSpec((main, width), rowblk(main_i, 0)),
        out_shape=jax.ShapeDtypeStruct((t, width), BF16),
        scratch_shapes=[pltpu.VMEM((main + 2 * halo, width), BF16),
                        pltpu.VMEM((2, width, main + 2 * halo), BF16)],
        compiler_params=_params("parallel", "parallel"),
        name="neighborhood_attention",
    )(proj, proj, proj, proj, vt, vt, vt, bias)


def _na_bias_table(rpb):
    qc = jnp.arange(GRID_W)
    kc = jnp.arange(GRID_W)
    cstart = jnp.clip(qc - NA_WIN_C // 2, 0, GRID_W - NA_WIN_C)
    valid = (kc[None, :] >= cstart[:, None]) & (kc[None, :] < cstart[:, None] + NA_WIN_C)
    dc = jnp.clip(kc[None, :] - qc[:, None], -(NA_WIN_C - 1), NA_WIN_C - 1) + (NA_WIN_C - 1)
    onehot = (dc[None] == jnp.arange(2 * NA_WIN_C - 1)[:, None, None]).astype(F32)
    toep = jnp.einsum('hrd,dqk->hrqk', rpb.astype(F32), onehot, precision=lax.Precision.HIGHEST)
    toep = jnp.where(valid[None, None], toep, NEG_BIG)
    nh, ndr = rpb.shape[0], rpb.shape[1]
    tab = toep.reshape(nh // 2, 2, ndr, GRID_W, GRID_W)
    tab = jnp.transpose(tab, (0, 2, 4, 1, 3))
    return tab.reshape(nh // 2, ndr * GRID_W, 2 * GRID_W)


def _ssd_decay_kernel(dt_ref, dtb_ref, alog_ref, cs_ref, cse_ref, rf_ref, rb_ref):
    L = dt_ref.shape[0]
    tri = (lax.broadcasted_iota(jnp.int32, (L, L), 0)
           >= lax.broadcasted_iota(jnp.int32, (L, L), 1)).astype(BF16)
    raw = dt_ref[...] + dtb_ref[...]
    dt = jnp.maximum(raw, 0.0) + jnp.log1p(jnp.exp(-jnp.abs(raw)))
    a = dt * (-jnp.exp(alog_ref[...])) * LOG2_E
    cs = _split3_dot(tri, a)
    cse = cs - a
    ldt = jnp.log2(dt)
    cs_ref[...] = cs
    cse_ref[...] = cse
    rf_ref[...] = (cs - ldt).T
    rb_ref[...] = (cse + ldt).T


def _ssd_decay(dt, dtb, alog):
    L = SSD_CHUNK
    t = dt.shape[0]
    col = pl.BlockSpec((L, LANES), lambda i: (i, 0))
    rowm = pl.BlockSpec((LANES, L), lambda i: (0, i))
    return pl.pallas_call(
        _ssd_decay_kernel,
        grid=(t // L,),
        in_specs=[col, _resident((1, LANES)), _resident((1, LANES))],
        out_specs=[col, col, rowm, rowm],
        out_shape=[jax.ShapeDtypeStruct((t, LANES), F32), jax.ShapeDtypeStruct((t, LANES), F32),
                   jax.ShapeDtypeStruct((LANES, t), F32), jax.ShapeDtypeStruct((LANES, t), F32)],
        compiler_params=_params("parallel"),
        name="ssd_decay",
    )(dt, dtb, alog)


def _ssd_kernel(x_ref, b_ref, c_ref, z_ref, cs_ref, cse_ref, rf_ref, rb_ref, dsk_ref, ng_ref,
                o_ref, hf, hb, hbs, *, nc, hpg, groups):
    p = pl.program_id(1)
    c = pl.program_id(2)
    L = x_ref.shape[0]
    assert L == SSD_STATE == LANES
    gw = hpg * SSD_HEADDIM
    npair = gw // LANES
    first = _first_half_mask((1, LANES))
    rj = lax.broadcasted_iota(jnp.int32, (L, L), 0)
    rl = lax.broadcasted_iota(jnp.int32, (L, L), 1)
    lower = rj >= rl

    @pl.when((p == 0) & (c == 0))
    def _init():
        hf[...] = jnp.zeros_like(hf)
        hb[...] = jnp.zeros_like(hb)

    cs = cs_ref[...]
    rows_b = rb_ref[...]
    tot = cs[L - 1:L, :]

    def pair_scale(vals):
        return jnp.where(first, vals[0], vals[1])

    def row(m, lane):
        return m[lane:lane + 1, :]

    def col_bcast(m, lane):
        return jnp.broadcast_to(m[:, lane:lane + 1], (L, LANES))

    @pl.when(p == 0)
    def _backward_states():
        ch = nc - 1 - c
        for g in range(groups):
            bt = b_ref[:, SSD_STATE * g:SSD_STATE * (g + 1)].astype(F32).T
            for pr in range(npair):
                idx = g * npair + pr
                xp = x_ref[:, LANES * idx:LANES * (idx + 1)]
                upd, dec = [], []
                for hh in range(2):
                    lb = 2 * hpg * g + hpg + 2 * pr + hh
                    srow = jnp.exp2(row(rows_b, lb))
                    upd.append(jnp.dot((bt * srow).astype(BF16), xp, preferred_element_type=F32))
                    dec.append(jnp.exp2(tot[:, lb:lb + 1]))
                hbs[ch, idx] = hb[idx].astype(BF16)
                hb[idx] = hb[idx] * pair_scale(dec) + pair_scale(upd)

    @pl.when(p == 1)
    def _forward():
        cse = cse_ref[...]
        rows_f = rf_ref[...]
        for g in range(groups):
            bm = b_ref[:, SSD_STATE * g:SSD_STATE * (g + 1)]
            cm = c_ref[:, SSD_STATE * g:SSD_STATE * (g + 1)]
            bt = bm.astype(F32).T
            cb = lax.dot_general(cm, bm, (((1,), (1,)), ((), ())), preferred_element_type=F32)
            cf32 = cm.astype(F32)
            ys = []
            for pr in range(npair):
                idx = g * npair + pr
                sl = slice(LANES * idx, LANES * (idx + 1))
                xp = x_ref[:, sl]
                hf_b = hf[idx].astype(BF16)
                hb_b = hbs[c, idx]
                res, upd, dec = [], [], []
                for hh in range(2):
                    lf = 2 * hpg * g + 2 * pr + hh
                    lb = lf + hpg
                    cfc = col_bcast(cs, lf)
                    cbc = col_bcast(cse, lb)
                    e = jnp.where(lower, cfc - row(rows_f, lf), row(rows_b, lb) - cbc)
                    m = (cb * jnp.exp2(e)).astype(BF16)
                    c_f = (cf32 * jnp.exp2(cfc)).astype(BF16)
                    c_b = (cf32 * jnp.exp2(tot[:, lb:lb + 1] - cbc)).astype(BF16)
                    res.append(jnp.dot(m, xp, preferred_element_type=F32)
                               + jnp.dot(c_f, hf_b, preferred_element_type=F32)
                               + jnp.dot(c_b, hb_b, preferred_element_type=F32))
                    srow = jnp.exp2(tot[:, lf:lf + 1] - row(rows_f, lf))
                    upd.append(jnp.dot((bt * srow).astype(BF16), xp, preferred_element_type=F32))
                    dec.append(jnp.exp2(tot[:, lf:lf + 1]))
                hf[idx] = hf[idx] * pair_scale(dec) + pair_scale(upd)
                y = pair_scale(res) + xp.astype(F32) * dsk_ref[:, sl]
                z = z_ref[:, sl].astype(F32)
                ys.append(y * (z * jax.nn.sigmoid(z)))
            y = jnp.concatenate(ys, axis=1)
            gsl = slice(gw * g, gw * (g + 1))
            o_ref[:, gsl] = _rms_rows(y, ng_ref[:, gsl]).astype(o_ref.dtype)


def _ssd_scan(xbc, z, decay, dsk, ng, batch, seq, inner, hpg):
    cs, cse, rows_f, rows_b = decay
    L = SSD_CHUNK
    nc = seq // L
    groups = inner // (hpg * SSD_HEADDIM)
    npairs = inner // LANES
    bc_w = groups * SSD_STATE
    t = xbc.shape[0]
    b_col = inner // bc_w

    def any_map(col):
        return lambda b, p, c: (b * nc + _chunk_of(p, c, nc), col)

    def fwd_map(col):
        return lambda b, p, c: (b * nc + p * c, col)

    any_t = lambda b, p, c: (0, b * nc + _chunk_of(p, c, nc))
    fwd_t = lambda b, p, c: (0, b * nc + p * c)
    return pl.pallas_call(
        functools.partial(_ssd_kernel, nc=nc, hpg=hpg, groups=groups),
        grid=(batch, 2, nc),
        in_specs=[pl.BlockSpec((L, inner), any_map(0)),
                  pl.BlockSpec((L, bc_w), any_map(b_col)),
                  pl.BlockSpec((L, bc_w), fwd_map(b_col + 1)),
                  pl.BlockSpec((L, inner), fwd_map(0)),
                  pl.BlockSpec((L, LANES), any_map(0)), pl.BlockSpec((L, LANES), fwd_map(0)),
                  pl.BlockSpec((LANES, L), fwd_t), pl.BlockSpec((LANES, L), any_t),
                  _resident((1, inner)), _resident((1, inner))],
        out_specs=pl.BlockSpec((L, inner), fwd_map(0)),
        out_shape=jax.ShapeDtypeStruct((t, inner), BF16),
        scratch_shapes=[pltpu.VMEM((npairs, SSD_STATE, LANES), F32),
                        pltpu.VMEM((npairs, SSD_STATE, LANES), F32),
                        pltpu.VMEM((nc, npairs, SSD_STATE, LANES), BF16)],
        compiler_params=_params("arbitrary", "arbitrary", "arbitrary"),
        name="ssd_scan",
    )(xbc, xbc, xbc, z, cs, cse, rows_f, rows_b, dsk, ng)


def _rotary_tables(seq):
    half = HEAD_DIM // 2
    inv = 1.0 / (ROPE_BASE ** (jnp.arange(half, dtype=F32) / half))
    ang = jnp.arange(seq).astype(F32)[:, None] * inv[None, :]
    cos = jnp.cos(ang)
    sin = jnp.sin(ang)
    reps = LANES // HEAD_DIM
    cos_t = jnp.tile(jnp.concatenate([cos, cos], axis=1), (1, reps))
    sin_t = jnp.tile(jnp.concatenate([-sin, sin], axis=1), (1, reps))
    return cos_t, sin_t


def _group_dt_lanes(v, hpg):
    lead = v.shape[:-2]
    heads = v.shape[-1]
    groups = heads // hpg
    assert 2 * heads <= LANES
    vg = v.reshape(*lead, 2, groups, hpg)
    vg = jnp.moveaxis(vg, -3, -2).reshape(*lead, 2 * heads)
    pad = [(0, 0)] * (vg.ndim - 1) + [(0, LANES - 2 * heads)]
    return jnp.pad(vg, pad)


def kernel(x, norm_mix_pre, norm_mix_post, norm_ffn_pre, norm_ffn_post, ab_w_in, ab_ret_decay_logit, ab_ret_gn_g, ab_na_rpb, ab_w_out, c_w_in, c_conv_w, c_conv_b, c_dt_bias, c_a_log, c_d_skip, c_norm_g, c_w_out, ffn_w_up, ffn_conv_w, ffn_conv_b, ffn_w_down):
    batch, seq, d = x.shape
    depth = norm_mix_pre.shape[0]
    t = batch * seq
    ret_w = RET_HEADS * HEAD_DIM
    na_w = NA_HEADS * HEAD_DIM
    inner = c_w_out.shape[1]
    heads = inner // SSD_HEADDIM
    hpg = heads // SSD_GROUPS
    xbc_w = inner + 2 * SSD_GROUPS * SSD_STATE
    assert ret_w == na_w and ab_w_in.shape[2] == 4 * ret_w + 3 * na_w

    h = x.reshape(t, d).astype(F32)
    cos_t, sin_t = _rotary_tables(seq)
    gmat = jnp.kron(jnp.eye(RET_HEADS, dtype=F32),
                    jnp.full((HEAD_DIM, HEAD_DIM), 1.0 / HEAD_DIM, F32)).astype(BF16)
    row = lambda v: v.reshape(1, -1).astype(F32)

    for layer in range(depth):
        i = layer // 2
        if layer % 2 == 0:
            n_tok = 4 * ret_w + 2 * na_w
            proj, na_vt = _norm_proj(h, row(norm_mix_pre[layer]),
                                     ab_w_in[i][:, :n_tok].astype(BF16),
                                     ab_w_in[i][:, n_tok:].T.astype(BF16))
            lg = -jax.nn.softplus(-ab_ret_decay_logit[i].astype(F32))
            lgf = jnp.repeat(lg[0], HEAD_DIM).reshape(1, ret_w)
            lgb = jnp.repeat(lg[1], HEAD_DIM).reshape(1, ret_w)
            ret = _retention(proj, cos_t, sin_t, lgf, lgb, row(ab_ret_gn_g[i]), gmat,
                             batch, seq, ret_w)
            na = _neighborhood_attention(proj, na_vt, _na_bias_table(ab_na_rpb[i]), batch, seq, na_w,
                                         4 * ret_w // na_w, 4 * ret_w // na_w + 1)
            h = _out_proj([ret, na], ab_w_out[i].astype(BF16), row(norm_mix_post[layer]), h)
        else:
            w_in = c_w_in[i]
            wz = w_in[:, :inner].astype(BF16)
            wx = w_in[:, inner:inner + xbc_w].astype(BF16)
            wdt = _group_dt_lanes(w_in[:, inner + xbc_w:].reshape(d, 2, heads), hpg).astype(BF16)
            z, xbc, dtr = _ssd_inproj(h, row(norm_mix_pre[layer]), wz, wx, wdt,
                                      c_conv_w[i].astype(F32), row(c_conv_b[i]), seq)
            dtb = _group_dt_lanes(c_dt_bias[i].astype(F32), hpg).reshape(1, -1)
            alog = _group_dt_lanes(c_a_log[i].astype(F32), hpg).reshape(1, -1)
            dsk = jnp.repeat(c_d_skip[i].astype(F32), SSD_HEADDIM).reshape(1, inner)
            y = _ssd_scan(xbc, z, _ssd_decay(dtr, dtb, alog), dsk, row(c_norm_g[i]),
                          batch, seq, inner, hpg)
            h = _out_proj([y], c_w_out[i].astype(BF16), row(norm_mix_post[layer]), h)
        h = _ffn(h, row(norm_ffn_pre[layer]), ffn_w_up[layer].astype(BF16),
                 ffn_conv_w[layer].astype(F32), row(ffn_conv_b[layer]),
                 ffn_w_down[layer].astype(BF16), row(norm_ffn_post[layer]), seq)
    return h.reshape(batch, seq, d).astype(x.dtype)
```

```python
import functools
import math

import jax
import jax.numpy as jnp
from jax import lax
from jax.experimental import pallas as pl
from jax.experimental.pallas import tpu as pltpu

F32 = jnp.float32
BF16 = jnp.bfloat16

EPS = 1e-6
GRID_W = 64
ROPE_BASE = 10000.0
RET_HEADS = 8
NA_HEADS = 8
NA_WIN_R = 8
NA_WIN_C = 16
SSD_HEADDIM = 64
SSD_GROUPS = 4
SSD_STATE = 128
HEAD_DIM = 64
LANES = 128
HALO = 8
RET_CHUNK = 256
SSD_CHUNK = 128
NA_ROWS_PER_STEP = 32
NA_ROW_UNROLL = 8
SSD_INPROJ_CONV_COLS = 1024
NEG_BIG = -1e30
LOG2_E = 1.0 / math.log(2.0)
VMEM_LIMIT = 56 * 1024 * 1024


def _params(*sem):
    return pltpu.CompilerParams(dimension_semantics=sem, vmem_limit_bytes=VMEM_LIMIT)


def _resident(shape):
    nd = len(shape)
    return pl.BlockSpec(shape, lambda *_: (0,) * nd, pipeline_mode=pl.Buffered(1))


def _rms_rows(x, g):
    return x * lax.rsqrt(jnp.mean(x * x, axis=-1, keepdims=True) + EPS) * g


def _split3_dot(tri, a):
    hi = a.astype(BF16)
    r1 = a - hi.astype(F32)
    mid = r1.astype(BF16)
    lo = (r1 - mid.astype(F32)).astype(BF16)
    return (jnp.dot(tri, hi, preferred_element_type=F32)
            + jnp.dot(tri, mid, preferred_element_type=F32)
            + jnp.dot(tri, lo, preferred_element_type=F32))


def _split2_dot(a, m):
    hi = a.astype(BF16)
    lo = (a - hi.astype(F32)).astype(BF16)
    return jnp.dot(hi, m, preferred_element_type=F32) + jnp.dot(lo, m, preferred_element_type=F32)


def _norm_proj_kernel(x_ref, g_ref, w_ref, wt_ref, o_ref, ot_ref):
    hn = _rms_rows(x_ref[...], g_ref[...]).astype(BF16)
    o_ref[...] = jnp.dot(hn, w_ref[...], preferred_element_type=F32).astype(o_ref.dtype)
    ot_ref[...] = lax.dot_general(wt_ref[...], hn, (((1,), (1,)), ((), ())),
                                  preferred_element_type=F32).astype(ot_ref.dtype)


def _norm_proj(x, g, w, wt, tm=1024):
    t, d = x.shape
    n = w.shape[1]
    nt = wt.shape[0]
    return pl.pallas_call(
        _norm_proj_kernel,
        grid=(t // tm,),
        in_specs=[pl.BlockSpec((tm, d), lambda i: (i, 0)), _resident((1, d)), _resident((d, n)),
                  _resident((nt, d))],
        out_specs=[pl.BlockSpec((tm, n), lambda i: (i, 0)), pl.BlockSpec((nt, tm), lambda i: (0, i))],
        out_shape=[jax.ShapeDtypeStruct((t, n), BF16), jax.ShapeDtypeStruct((nt, t), BF16)],
        compiler_params=_params("parallel"),
        name="norm_proj",
    )(x, g, w, wt)


def _out_proj_kernel(*refs, n_act):
    acts = refs[:n_act]
    w_ref, g_ref, x_ref, o_ref = refs[n_act:]
    k0 = 0
    m = None
    for a in acts:
        k = a.shape[1]
        part = jnp.dot(a[...], w_ref[k0:k0 + k, :], preferred_element_type=F32)
        m = part if m is None else m + part
        k0 += k
    o_ref[...] = x_ref[...] + _rms_rows(m, g_ref[...])


def _out_proj(acts, w, g, x, tm=1024):
    t, d = x.shape
    in_specs = [pl.BlockSpec((tm, a.shape[1]), lambda i: (i, 0)) for a in acts]
    in_specs += [_resident(w.shape), _resident((1, d)), pl.BlockSpec((tm, d), lambda i: (i, 0))]
    return pl.pallas_call(
        functools.partial(_out_proj_kernel, n_act=len(acts)),
        grid=(t // tm,),
        in_specs=in_specs,
        out_specs=pl.BlockSpec((tm, d), lambda i: (i, 0)),
        out_shape=jax.ShapeDtypeStruct((t, d), F32),
        compiler_params=_params("parallel"),
        name="out_proj",
    )(*acts, w, g, x)


def _halo_specs(tm, d, t):
    nb = tm // HALO
    last = t // HALO - 1
    return [
        pl.BlockSpec((HALO, d), lambda i: (jnp.maximum(i * nb - 1, 0), 0)),
        pl.BlockSpec((tm, d), lambda i: (i, 0)),
        pl.BlockSpec((HALO, d), lambda i: (jnp.minimum((i + 1) * nb, last), 0)),
    ]


def _normed_tile_with_halo(xp_ref, x_ref, xn_ref, g_ref, tiles_per_seq):
    i = pl.program_id(0)
    pos = i % tiles_per_seq
    has_prev = (pos != 0).astype(F32)
    has_next = (pos != tiles_per_seq - 1).astype(F32)
    g = g_ref[...]
    top = _rms_rows(xp_ref[...], g) * has_prev
    mid = _rms_rows(x_ref[...], g)
    bot = _rms_rows(xn_ref[...], g) * has_next
    return jnp.concatenate([top, mid, bot], axis=0).astype(BF16), mid.astype(BF16)


def _depthwise_conv_rows(u, w_ref, b_ref, cols, tm):
    n = u.shape[0]
    width = w_ref.shape[0]
    pad = width // 2
    acc = None
    for k in range(width):
        shift = pad - k
        us = u if shift == 0 else pltpu.roll(u, shift % n, 0)
        term = us[HALO:HALO + tm] * w_ref[k:k + 1, cols]
        acc = term if acc is None else acc + term
    return acc + b_ref[:, cols]


def _ffn_kernel(xp_ref, x_ref, xn_ref, gpre_ref, wup_ref, cw_ref, cb_ref, wdn_ref, gpost_ref,
                o_ref, *, tiles_per_seq):
    tm = x_ref.shape[0]
    f = wdn_ref.shape[0]
    hn, _ = _normed_tile_with_halo(xp_ref, x_ref, xn_ref, gpre_ref, tiles_per_seq)
    gcols = slice(0, f)
    vcols = slice(f, 2 * f)
    ug = jnp.dot(hn, wup_ref[:, gcols], preferred_element_type=F32)
    uv = jnp.dot(hn, wup_ref[:, vcols], preferred_element_type=F32)
    gate = _depthwise_conv_rows(ug, cw_ref, cb_ref, gcols, tm)
    val = _depthwise_conv_rows(uv, cw_ref, cb_ref, vcols, tm)
    h = (jax.nn.gelu(gate, approximate=True) * val).astype(BF16)
    m = jnp.dot(h, wdn_ref[...], preferred_element_type=F32)
    o_ref[...] = x_ref[...] + _rms_rows(m, gpost_ref[...])


def _ffn(x, gpre, wup, cw, cb, wdn, gpost, seq, tm=512):
    t, d = x.shape
    f = wdn.shape[0]
    return pl.pallas_call(
        functools.partial(_ffn_kernel, tiles_per_seq=seq // tm),
        grid=(t // tm,),
        in_specs=_halo_specs(tm, d, t) + [
            _resident((1, d)), _resident(wup.shape), _resident(cw.shape), _resident((1, 2 * f)),
            _resident(wdn.shape), _resident((1, d))],
        out_specs=pl.BlockSpec((tm, d), lambda i: (i, 0)),
        out_shape=jax.ShapeDtypeStruct((t, d), F32),
        compiler_params=_params("parallel"),
        name="ffn",
    )(x, x, x, gpre, wup, cw, cb, wdn, gpost)


def _ssd_inproj_kernel(xp_ref, x_ref, xn_ref, g_ref, wz_ref, wx_ref, wdt_ref, cw_ref, cb_ref,
                       z_ref, xbc_ref, dt_ref, *, tiles_per_seq, ncol):
    tm = x_ref.shape[0]
    hn, hm = _normed_tile_with_halo(xp_ref, x_ref, xn_ref, g_ref, tiles_per_seq)
    z_ref[...] = jnp.dot(hm, wz_ref[...], preferred_element_type=F32).astype(z_ref.dtype)
    dt_ref[...] = jnp.dot(hm, wdt_ref[...], preferred_element_type=F32)
    for c in range(0, wx_ref.shape[1], ncol):
        cols = slice(c, c + ncol)
        u = jnp.dot(hn, wx_ref[:, cols], preferred_element_type=F32)
        v = _depthwise_conv_rows(u, cw_ref, cb_ref, cols, tm)
        xbc_ref[:, cols] = (v * jax.nn.sigmoid(v)).astype(xbc_ref.dtype)


def _ssd_inproj(x, g, wz, wx, wdt, cw, cb, seq, tm=512, ncol=SSD_INPROJ_CONV_COLS):
    t, d = x.shape
    nz, nx, ndt = wz.shape[1], wx.shape[1], wdt.shape[1]
    row = lambda n: pl.BlockSpec((tm, n), lambda i: (i, 0))
    return pl.pallas_call(
        functools.partial(_ssd_inproj_kernel, tiles_per_seq=seq // tm, ncol=ncol),
        grid=(t // tm,),
        in_specs=_halo_specs(tm, d, t) + [
            _resident((1, d)), _resident(wz.shape), _resident(wx.shape), _resident(wdt.shape),
            _resident(cw.shape), _resident((1, nx))],
        out_specs=[row(nz), row(nx), row(ndt)],
        out_shape=[jax.ShapeDtypeStruct((t, nz), BF16), jax.ShapeDtypeStruct((t, nx), BF16),
                   jax.ShapeDtypeStruct((t, ndt), F32)],
        compiler_params=_params("parallel"),
        name="ssd_inproj",
    )(x, x, x, g, wz, wx, wdt, cw, cb)


def _chunk_of(p, c, nc):
    return p * c + (1 - p) * (nc - 1 - c)


def _first_half_mask(shape):
    return lax.broadcasted_iota(jnp.int32, shape, len(shape) - 1) < HEAD_DIM


def _ret_kernel(q_ref, k_ref, v_ref, g_ref, cos_ref, sin_ref, lgf_ref, lgb_ref, gn_ref, gm_ref,
                o_ref, hf, hb, hbs, dm, ptab, *, nc):
    p = pl.program_id(1)
    c = pl.program_id(2)
    L = k_ref.shape[0]
    npair = k_ref.shape[1] // LANES
    lgf = lgf_ref[...]
    lgb = lgb_ref[...]
    first = _first_half_mask((1, LANES))
    swap_up = (lax.broadcasted_iota(jnp.int32, (1, LANES), 1) % HEAD_DIM) < HEAD_DIM // 2
    ri = lax.broadcasted_iota(jnp.int32, (LANES, LANES), 0) < HEAD_DIM
    ci = lax.broadcasted_iota(jnp.int32, (LANES, LANES), 1) < HEAD_DIM
    same_head = ri == ci
    cos_f = cos_ref[...]
    sin_s = sin_ref[...]
    scale = HEAD_DIM ** -0.5
    TAIL_B, DEC_F, DEC_B, TAIL_F = range(4)

    def rot(x):
        sw = jnp.where(swap_up, pltpu.roll(x, LANES - HEAD_DIM // 2, 1),
                       pltpu.roll(x, HEAD_DIM // 2, 1))
        return x * cos_f + sw * sin_s

    @pl.when((p == 0) & (c == 0))
    def _init():
        hf[...] = jnp.zeros_like(hf)
        hb[...] = jnp.zeros_like(hb)
        d = (lax.broadcasted_iota(jnp.int32, (L, L), 0)
             - lax.broadcasted_iota(jnp.int32, (L, L), 1)).astype(F32)
        for h in range(2 * npair):
            lf = lgf[:, HEAD_DIM * h:HEAD_DIM * h + 1]
            lb = lgb[:, HEAD_DIM * h:HEAD_DIM * h + 1]
            dm[h] = jnp.exp(jnp.where(d >= 0, d * lf, -d * lb))
        jj = lax.broadcasted_iota(jnp.int32, (L, 1), 0).astype(F32)
        ptab[TAIL_B] = jnp.exp(jj * lgb)
        ptab[DEC_F] = jnp.exp((jj + 1.0) * lgf)
        ptab[DEC_B] = jnp.exp((L - jj) * lgb)
        ptab[TAIL_F] = jnp.exp((L - 1.0 - jj) * lgf)

    @pl.when(p == 0)
    def _backward_states():
        ch = nc - 1 - c
        k = k_ref[...].astype(F32)
        decay_b = jnp.exp(L * lgb)
        for pr in range(npair):
            sl = slice(LANES * pr, LANES * (pr + 1))
            kp = rot(k[:, sl]) * scale
            kt = (kp * ptab[TAIL_B, :, sl]).T.astype(BF16)
            hbs[ch, pr] = hb[pr].astype(BF16)
            upd = jnp.dot(kt, v_ref[:, sl], preferred_element_type=F32)
            hb[pr] = jnp.where(same_head, hb[pr] * decay_b[:, sl] + upd, 0.0)

    @pl.when(p == 1)
    def _forward():
        q = q_ref[...].astype(F32)
        k = k_ref[...].astype(F32)
        decay_f = jnp.exp(L * lgf)
        ys = []
        for pr in range(npair):
            sl = slice(LANES * pr, LANES * (pr + 1))
            qp = rot(q[:, sl])
            kp = rot(k[:, sl]) * scale
            kb = kp.astype(BF16)
            vp = v_ref[:, sl]
            res = []
            for hh in range(2):
                keep = first if hh == 0 else jnp.logical_not(first)
                qm = jnp.where(keep, qp, 0.0).astype(BF16)
                s = lax.dot_general(qm, kb, (((1,), (1,)), ((), ())), preferred_element_type=F32)
                a = (s * dm[2 * pr + hh]).astype(BF16)
                res.append(jnp.dot(a, vp, preferred_element_type=F32))
            y = jnp.where(first, res[0], res[1])
            y = y + jnp.dot((qp * ptab[DEC_F, :, sl]).astype(BF16), hf[pr].astype(BF16),
                            preferred_element_type=F32)
            y = y + jnp.dot((qp * ptab[DEC_B, :, sl]).astype(BF16), hbs[c, pr],
                            preferred_element_type=F32)
            kt = (kp * ptab[TAIL_F, :, sl]).T.astype(BF16)
            upd = jnp.dot(kt, vp, preferred_element_type=F32)
            hf[pr] = jnp.where(same_head, hf[pr] * decay_f[:, sl] + upd, 0.0)
            ys.append(y)
        y = jnp.concatenate(ys, axis=1)
        gm = gm_ref[...]
        mu = _split2_dot(y, gm)
        d = y - mu
        var = _split2_dot(d * d, gm)
        yn = d * lax.rsqrt(var + EPS) * gn_ref[...]
        g = g_ref[...].astype(F32)
        o_ref[...] = (g * jax.nn.sigmoid(g) * yn).astype(o_ref.dtype)


def _retention(proj, cos_t, sin_t, lgf, lgb, gn, gmat, batch, seq, width):
    L = RET_CHUNK
    nc = seq // L
    nw = width // LANES
    t = proj.shape[0]

    def kv_map(col):
        return lambda b, p, c: (b * nc + _chunk_of(p, c, nc), col)

    def fwd_map(col):
        return lambda b, p, c: (b * nc + p * c, col)

    tab_map = lambda b, p, c: (_chunk_of(p, c, nc), 0)
    return pl.pallas_call(
        functools.partial(_ret_kernel, nc=nc),
        grid=(batch, 2, nc),
        in_specs=[pl.BlockSpec((L, width), fwd_map(0)), pl.BlockSpec((L, width), kv_map(1)),
                  pl.BlockSpec((L, width), kv_map(2)), pl.BlockSpec((L, width), fwd_map(3)),
                  pl.BlockSpec((L, LANES), tab_map), pl.BlockSpec((L, LANES), tab_map),
                  _resident((1, width)), _resident((1, width)), _resident((1, width)),
                  _resident((width, width))],
        out_specs=pl.BlockSpec((L, width), fwd_map(0)),
        out_shape=jax.ShapeDtypeStruct((t, width), BF16),
        scratch_shapes=[pltpu.VMEM((nw, LANES, LANES), F32), pltpu.VMEM((nw, LANES, LANES), F32),
                        pltpu.VMEM((nc, nw, LANES, LANES), BF16), pltpu.VMEM((2 * nw, L, L), F32),
                        pltpu.VMEM((4, L, width), F32)],
        compiler_params=_params("arbitrary", "arbitrary", "arbitrary"),
        name="retention",
    )(proj, proj, proj, proj, cos_t, sin_t, lgf, lgb, gn, gmat)


def _na_kernel(q_ref, kp_ref, kc_ref, kn_ref, vp_ref, vc_ref, vn_ref, bias_ref, o_ref,
               kbuf, vtb, *, rows_total):
    i = pl.program_id(1)
    halo = kp_ref.shape[0]
    main = kc_ref.shape[0]
    span = main + 2 * halo
    nkeys = NA_WIN_R * GRID_W
    npair = o_ref.shape[1] // LANES
    kbuf[0:halo] = kp_ref[...]
    kbuf[halo:halo + main] = kc_ref[...]
    kbuf[halo + main:] = kn_ref[...]
    vtb[0, :, 0:halo] = vp_ref[...]
    vtb[0, :, halo:halo + main] = vc_ref[...]
    vtb[0, :, halo + main:] = vn_ref[...]
    vtb[1] = pltpu.roll(vtb[0], span - GRID_W, 1)
    first = _first_half_mask((1, LANES))
    scale = HEAD_DIM ** -0.5
    buf_row0 = i * NA_ROWS_PER_STEP - NA_WIN_R // 2

    def row_body(r, carry):
        gr = i * NA_ROWS_PER_STEP + r
        r0 = jnp.clip(gr - NA_WIN_R // 2, 0, rows_total - NA_WIN_R)
        boff = pl.multiple_of((NA_WIN_R - 1 - (gr - r0)) * GRID_W, GRID_W)
        rel = r0 - buf_row0
        odd = rel % 2
        off = pl.multiple_of(rel * GRID_W, GRID_W)
        base = pl.multiple_of((rel - odd) * GRID_W, LANES)
        qoff = pl.multiple_of(r * GRID_W, GRID_W)
        qrow = q_ref[pl.ds(qoff, GRID_W), :].astype(F32) * scale
        outs = []
        for pr in range(npair):
            sl = slice(LANES * pr, LANES * (pr + 1))
            qp = qrow[:, sl]
            qm = jnp.concatenate([jnp.where(first, qp, 0.0), jnp.where(first, 0.0, qp)],
                                 axis=0).astype(BF16)
            kwin = kbuf[pl.ds(off, nkeys), sl]
            st = lax.dot_general(kwin, qm, (((1,), (1,)), ((), ())), preferred_element_type=F32)
            st = st + bias_ref[pr, pl.ds(boff, nkeys), :]
            m = jnp.max(st, axis=0, keepdims=True)
            e = jnp.exp(st - m)
            l = jnp.sum(e, axis=0, keepdims=True)
            vtw = vtb[odd, sl, pl.ds(base, nkeys)]
            ot = jnp.dot(vtw, e.astype(BF16), preferred_element_type=F32) / l
            o = ot.T
            outs.append(jnp.where(first, o[0:GRID_W], o[GRID_W:2 * GRID_W]))
        o_ref[pl.ds(qoff, GRID_W), :] = jnp.concatenate(outs, axis=1).astype(o_ref.dtype)
        return carry

    lax.fori_loop(0, NA_ROWS_PER_STEP, row_body, 0, unroll=NA_ROW_UNROLL)


def _neighborhood_attention(proj, vt, bias, batch, seq, width, q_col, k_col):
    assert 2 * GRID_W == LANES
    rows = seq // GRID_W
    nstep = rows // NA_ROWS_PER_STEP
    main = NA_ROWS_PER_STEP * GRID_W
    halo = (NA_WIN_R // 2) * GRID_W
    per = main // halo
    nhalo = seq // halo
    t = proj.shape[0]

    main_i = lambda b, i: b * nstep + i
    prev_i = lambda b, i: b * nhalo + jnp.maximum(i * per - 1, 0)
    next_i = lambda b, i: b * nhalo + jnp.minimum((i + 1) * per, nhalo - 1)
    rowblk = lambda f, col: (lambda b, i: (f(b, i), col))
    colblk = lambda f: (lambda b, i: (0, f(b, i)))

    return pl.pallas_call(
        functools.partial(_na_kernel, rows_total=rows),
        grid=(batch, nstep),
        in_specs=[pl.BlockSpec((main, width), rowblk(main_i, q_col)),
                  pl.BlockSpec((halo, width), rowblk(prev_i, k_col)),
                  pl.BlockSpec((main, width), rowblk(main_i, k_col)),
                  pl.BlockSpec((halo, width), rowblk(next_i, k_col)),
                  pl.BlockSpec((width, halo), colblk(prev_i)),
                  pl.BlockSpec((width, main), colblk(main_i)),
                  pl.BlockSpec((width, halo), colblk(next_i)),
                  _resident(bias.shape)],
        out_specs=pl.BlockSpec((main, width), rowblk(main_i, 0)),
        out_shape=jax.ShapeDtypeStruct((t, width), BF16),
        scratch_shapes=[pltpu.VMEM((main + 2 * halo, width), BF16),
                        pltpu.VMEM((2, width, main + 2 * halo), BF16)],
        compiler_params=_params("parallel", "parallel"),
        name="neighborhood_attention",
    )(proj, proj, proj, proj, vt, vt, vt, bias)


def _na_bias_table(rpb):
    qc = jnp.arange(GRID_W)
    kc = jnp.arange(GRID_W)
    cstart = jnp.clip(qc - NA_WIN_C // 2, 0, GRID_W - NA_WIN_C)
    valid = (kc[None, :] >= cstart[:, None]) & (kc[None, :] < cstart[:, None] + NA_WIN_C)
    dc = jnp.clip(kc[None, :] - qc[:, None], -(NA_WIN_C - 1), NA_WIN_C - 1) + (NA_WIN_C - 1)
    onehot = (dc[None] == jnp.arange(2 * NA_WIN_C - 1)[:, None, None]).astype(F32)
    toep = jnp.einsum('hrd,dqk->hrqk', rpb.astype(F32), onehot, precision=lax.Precision.HIGHEST)
    toep = jnp.where(valid[None, None], toep, NEG_BIG)
    nh, ndr = rpb.shape[0], rpb.shape[1]
    tab = toep.reshape(nh // 2, 2, ndr, GRID_W, GRID_W)
    tab = jnp.transpose(tab, (0, 2, 4, 1, 3))
    return tab.reshape(nh // 2, ndr * GRID_W, 2 * GRID_W)


def _ssd_decay_kernel(dt_ref, dtb_ref, alog_ref, cs_ref, cse_ref, rf_ref, rb_ref):
    L = SSD_CHUNK
    tri = (lax.broadcasted_iota(jnp.int32, (L, L), 0)
           >= lax.broadcasted_iota(jnp.int32, (L, L), 1)).astype(BF16)
    raw = dt_ref[...] + dtb_ref[...]
    dt = jnp.maximum(raw, 0.0) + jnp.log1p(jnp.exp(-jnp.abs(raw)))
    a = dt * (-jnp.exp(alog_ref[...])) * LOG2_E
    ldt = jnp.log2(dt)
    for k in range(dt_ref.shape[0] // L):
        rows = slice(k * L, (k + 1) * L)
        cs = _split3_dot(tri, a[rows])
        cse = cs - a[rows]
        cs_ref[rows, :] = cs
        cse_ref[rows, :] = cse
        rf_ref[:, rows] = (cs - ldt[rows]).T
        rb_ref[:, rows] = (cse + ldt[rows]).T


def _ssd_decay(dt, dtb, alog, chunks_per_step=8):
    L = SSD_CHUNK * chunks_per_step
    t = dt.shape[0]
    col = pl.BlockSpec((L, LANES), lambda i: (i, 0))
    rowm = pl.BlockSpec((LANES, L), lambda i: (0, i))
    return pl.pallas_call(
        _ssd_decay_kernel,
        grid=(t // L,),
        in_specs=[col, _resident((1, LANES)), _resident((1, LANES))],
        out_specs=[col, col, rowm, rowm],
        out_shape=[jax.ShapeDtypeStruct((t, LANES), F32), jax.ShapeDtypeStruct((t, LANES), F32),
                   jax.ShapeDtypeStruct((LANES, t), F32), jax.ShapeDtypeStruct((LANES, t), F32)],
        compiler_params=_params("parallel"),
        name="ssd_decay",
    )(dt, dtb, alog)


def _ssd_kernel(x_ref, b_ref, c_ref, z_ref, cs_ref, cse_ref, rf_ref, rb_ref, dsk_ref, ng_ref,
                o_ref, hf, hb, hbs, *, nc, hpg, groups):
    p = pl.program_id(1)
    c = pl.program_id(2)
    L = x_ref.shape[0]
    assert L == SSD_STATE == LANES
    gw = hpg * SSD_HEADDIM
    npair = gw // LANES
    first = _first_half_mask((1, LANES))
    rj = lax.broadcasted_iota(jnp.int32, (L, L), 0)
    rl = lax.broadcasted_iota(jnp.int32, (L, L), 1)
    lower = rj >= rl

    @pl.when((p == 0) & (c == 0))
    def _init():
        hf[...] = jnp.zeros_like(hf)
        hb[...] = jnp.zeros_like(hb)

    cs = cs_ref[...]
    rows_b = rb_ref[...]
    tot = cs[L - 1:L, :]

    def pair_scale(vals):
        return jnp.where(first, vals[0], vals[1])

    def row(m, lane):
        return m[lane:lane + 1, :]

    def col_bcast(m, lane):
        return jnp.broadcast_to(m[:, lane:lane + 1], (L, LANES))

    @pl.when(p == 0)
    def _backward_states():
        ch = nc - 1 - c
        for g in range(groups):
            bt = b_ref[:, SSD_STATE * g:SSD_STATE * (g + 1)].astype(F32).T
            for pr in range(npair):
                idx = g * npair + pr
                xp = x_ref[:, LANES * idx:LANES * (idx + 1)]
                upd, dec = [], []
                for hh in range(2):
                    lb = 2 * hpg * g + hpg + 2 * pr + hh
                    srow = jnp.exp2(row(rows_b, lb))
                    upd.append(jnp.dot((bt * srow).astype(BF16), xp, preferred_element_type=F32))
                    dec.append(jnp.exp2(tot[:, lb:lb + 1]))
                hbs[ch, idx] = hb[idx].astype(BF16)
                hb[idx] = hb[idx] * pair_scale(dec) + pair_scale(upd)

    @pl.when(p == 1)
    def _forward():
        cse = cse_ref[...]
        rows_f = rf_ref[...]
        for g in range(groups):
            bm = b_ref[:, SSD_STATE * g:SSD_STATE * (g + 1)]
            cm = c_ref[:, SSD_STATE * g:SSD_STATE * (g + 1)]
            bt = bm.astype(F32).T
            cb = lax.dot_general(cm, bm, (((1,), (1,)), ((), ())), preferred_element_type=F32)
            cf32 = cm.astype(F32)
            ys = []
            for pr in range(npair):
                idx = g * npair + pr
                sl = slice(LANES * idx, LANES * (idx + 1))
                xp = x_ref[:, sl]
                hf_b = hf[idx].astype(BF16)
                hb_b = hbs[c, idx]
                res, upd, dec = [], [], []
                for hh in range(2):
                    lf = 2 * hpg * g + 2 * pr + hh
                    lb = lf + hpg
                    cfc = col_bcast(cs, lf)
                    cbc = col_bcast(cse, lb)
                    e = jnp.where(lower, cfc - row(rows_f, lf), row(rows_b, lb) - cbc)
                    m = (cb * jnp.exp2(e)).astype(BF16)
                    c_f = (cf32 * jnp.exp2(cfc)).astype(BF16)
                    c_b = (cf32 * jnp.exp2(tot[:, lb:lb + 1] - cbc)).astype(BF16)
                    res.append(jnp.dot(m, xp, preferred_element_type=F32)
                               + jnp.dot(c_f, hf_b, preferred_element_type=F32)
                               + jnp.dot(c_b, hb_b, preferred_element_type=F32))
                    srow = jnp.exp2(tot[:, lf:lf + 1] - row(rows_f, lf))
                    upd.append(jnp.dot((bt * srow).astype(BF16), xp, preferred_element_type=F32))
                    dec.append(jnp.exp2(tot[:, lf:lf + 1]))
                hf[idx] = hf[idx] * pair_scale(dec) + pair_scale(upd)
                y = pair_scale(res) + xp.astype(F32) * dsk_ref[:, sl]
                z = z_ref[:, sl].astype(F32)
                ys.append(y * (z * jax.nn.sigmoid(z)))
            y = jnp.concatenate(ys, axis=1)
            gsl = slice(gw * g, gw * (g + 1))
            o_ref[:, gsl] = _rms_rows(y, ng_ref[:, gsl]).astype(o_ref.dtype)


def _ssd_scan(xbc, z, decay, dsk, ng, batch, seq, inner, hpg):
    cs, cse, rows_f, rows_b = decay
    L = SSD_CHUNK
    nc = seq // L
    groups = inner // (hpg * SSD_HEADDIM)
    npairs = inner // LANES
    bc_w = groups * SSD_STATE
    t = xbc.shape[0]
    b_col = inner // bc_w

    def any_map(col):
        return lambda b, p, c: (b * nc + _chunk_of(p, c, nc), col)

    def fwd_map(col):
        return lambda b, p, c: (b * nc + p * c, col)

    any_t = lambda b, p, c: (0, b * nc + _chunk_of(p, c, nc))
    fwd_t = lambda b, p, c: (0, b * nc + p * c)
    return pl.pallas_call(
        functools.partial(_ssd_kernel, nc=nc, hpg=hpg, groups=groups),
        grid=(batch, 2, nc),
        in_specs=[pl.BlockSpec((L, inner), any_map(0)),
                  pl.BlockSpec((L, bc_w), any_map(b_col)),
                  pl.BlockSpec((L, bc_w), fwd_map(b_col + 1)),
                  pl.BlockSpec((L, inner), fwd_map(0)),
                  pl.BlockSpec((L, LANES), any_map(0)), pl.BlockSpec((L, LANES), fwd_map(0)),
                  pl.BlockSpec((LANES, L), fwd_t), pl.BlockSpec((LANES, L), any_t),
                  _resident((1, inner)), _resident((1, inner))],
        out_specs=pl.BlockSpec((L, inner), fwd_map(0)),
        out_shape=jax.ShapeDtypeStruct((t, inner), BF16),
        scratch_shapes=[pltpu.VMEM((npairs, SSD_STATE, LANES), F32),
                        pltpu.VMEM((npairs, SSD_STATE, LANES), F32),
                        pltpu.VMEM((nc, npairs, SSD_STATE, LANES), BF16)],
        compiler_params=_params("arbitrary", "arbitrary", "arbitrary"),
        name="ssd_scan",
    )(xbc, xbc, xbc, z, cs, cse, rows_f, rows_b, dsk, ng)


def _rotary_tables(seq):
    half = HEAD_DIM // 2
    inv = 1.0 / (ROPE_BASE ** (jnp.arange(half, dtype=F32) / half))
    ang = jnp.arange(seq).astype(F32)[:, None] * inv[None, :]
    cos = jnp.cos(ang)
    sin = jnp.sin(ang)
    reps = LANES // HEAD_DIM
    cos_t = jnp.tile(jnp.concatenate([cos, cos], axis=1), (1, reps))
    sin_t = jnp.tile(jnp.concatenate([-sin, sin], axis=1), (1, reps))
    return cos_t, sin_t


def _group_dt_lanes(v, hpg):
    lead = v.shape[:-2]
    heads = v.shape[-1]
    groups = heads // hpg
    assert 2 * heads <= LANES
    vg = v.reshape(*lead, 2, groups, hpg)
    vg = jnp.moveaxis(vg, -3, -2).reshape(*lead, 2 * heads)
    pad = [(0, 0)] * (vg.ndim - 1) + [(0, LANES - 2 * heads)]
    return jnp.pad(vg, pad)


def kernel(x, norm_mix_pre, norm_mix_post, norm_ffn_pre, norm_ffn_post, ab_w_in, ab_ret_decay_logit, ab_ret_gn_g, ab_na_rpb, ab_w_out, c_w_in, c_conv_w, c_conv_b, c_dt_bias, c_a_log, c_d_skip, c_norm_g, c_w_out, ffn_w_up, ffn_conv_w, ffn_conv_b, ffn_w_down):
    batch, seq, d = x.shape
    depth = norm_mix_pre.shape[0]
    t = batch * seq
    ret_w = RET_HEADS * HEAD_DIM
    na_w = NA_HEADS * HEAD_DIM
    inner = c_w_out.shape[1]
    heads = inner // SSD_HEADDIM
    hpg = heads // SSD_GROUPS
    xbc_w = inner + 2 * SSD_GROUPS * SSD_STATE
    assert ret_w == na_w and ab_w_in.shape[2] == 4 * ret_w + 3 * na_w

    h = x.reshape(t, d).astype(F32)
    cos_t, sin_t = _rotary_tables(seq)
    gmat = jnp.kron(jnp.eye(RET_HEADS, dtype=F32),
                    jnp.full((HEAD_DIM, HEAD_DIM), 1.0 / HEAD_DIM, F32)).astype(BF16)
    row = lambda v: v.reshape(1, -1).astype(F32)

    for layer in range(depth):
        i = layer // 2
        if layer % 2 == 0:
            n_tok = 4 * ret_w + 2 * na_w
            proj, na_vt = _norm_proj(h, row(norm_mix_pre[layer]),
                                     ab_w_in[i][:, :n_tok].astype(BF16),
                                     ab_w_in[i][:, n_tok:].T.astype(BF16))
            lg = -jax.nn.softplus(-ab_ret_decay_logit[i].astype(F32))
            lgf = jnp.repeat(lg[0], HEAD_DIM).reshape(1, ret_w)
            lgb = jnp.repeat(lg[1], HEAD_DIM).reshape(1, ret_w)
            ret = _retention(proj, cos_t, sin_t, lgf, lgb, row(ab_ret_gn_g[i]), gmat,
                             batch, seq, ret_w)
            na = _neighborhood_attention(proj, na_vt, _na_bias_table(ab_na_rpb[i]), batch, seq, na_w,
                                         4 * ret_w // na_w, 4 * ret_w // na_w + 1)
            h = _out_proj([ret, na], ab_w_out[i].astype(BF16), row(norm_mix_post[layer]), h)
        else:
            w_in = c_w_in[i]
            wz = w_in[:, :inner].astype(BF16)
            wx = w_in[:, inner:inner + xbc_w].astype(BF16)
            wdt = _group_dt_lanes(w_in[:, inner + xbc_w:].reshape(d, 2, heads), hpg).astype(BF16)
            z, xbc, dtr = _ssd_inproj(h, row(norm_mix_pre[layer]), wz, wx, wdt,
                                      c_conv_w[i].astype(F32), row(c_conv_b[i]), seq)
            dtb = _group_dt_lanes(c_dt_bias[i].astype(F32), hpg).reshape(1, -1)
            alog = _group_dt_lanes(c_a_log[i].astype(F32), hpg).reshape(1, -1)
            dsk = jnp.repeat(c_d_skip[i].astype(F32), SSD_HEADDIM).reshape(1, inner)
            y = _ssd_scan(xbc, z, _ssd_decay(dtr, dtb, alog), dsk, row(c_norm_g[i]),
                          batch, seq, inner, hpg)
            h = _out_proj([y], c_w_out[i].astype(BF16), row(norm_mix_post[layer]), h)
        h = _ffn(h, row(norm_ffn_pre[layer]), ffn_w_up[layer].astype(BF16),
                 ffn_conv_w[layer].astype(F32), row(ffn_conv_b[layer]),
                 ffn_w_down[layer].astype(BF16), row(norm_ffn_post[layer]), seq)
    return h.reshape(batch, seq, d).astype(x.dtype)
```

```python
import functools
import math

import jax
import jax.numpy as jnp
from jax import lax
from jax.experimental import pallas as pl
from jax.experimental.pallas import tpu as pltpu

F32 = jnp.float32
BF16 = jnp.bfloat16

EPS = 1e-6
GRID_W = 64
ROPE_BASE = 10000.0
RET_HEADS = 8
NA_HEADS = 8
NA_WIN_R = 8
NA_WIN_C = 16
SSD_HEADDIM = 64
SSD_GROUPS = 4
SSD_STATE = 128
HEAD_DIM = 64
LANES = 128
HALO = 8
RET_CHUNK = 256
SSD_CHUNK = 128
SSD_CHUNKS_PER_STEP = 4
NA_ROWS_PER_STEP = 32
NA_ROW_UNROLL = 8
SSD_INPROJ_CONV_COLS = 1024
NEG_BIG = -1e30
LOG2_E = 1.0 / math.log(2.0)
VMEM_LIMIT = 56 * 1024 * 1024


def _params(*sem):
    return pltpu.CompilerParams(dimension_semantics=sem, vmem_limit_bytes=VMEM_LIMIT)


def _resident(shape):
    nd = len(shape)
    return pl.BlockSpec(shape, lambda *_: (0,) * nd, pipeline_mode=pl.Buffered(1))


def _rms_rows(x, g):
    return x * lax.rsqrt(jnp.mean(x * x, axis=-1, keepdims=True) + EPS) * g


def _split3_dot(tri, a):
    hi = a.astype(BF16)
    r1 = a - hi.astype(F32)
    mid = r1.astype(BF16)
    lo = (r1 - mid.astype(F32)).astype(BF16)
    return (jnp.dot(tri, hi, preferred_element_type=F32)
            + jnp.dot(tri, mid, preferred_element_type=F32)
            + jnp.dot(tri, lo, preferred_element_type=F32))


def _split2_dot(a, m):
    hi = a.astype(BF16)
    lo = (a - hi.astype(F32)).astype(BF16)
    return jnp.dot(hi, m, preferred_element_type=F32) + jnp.dot(lo, m, preferred_element_type=F32)


def _norm_proj_kernel(x_ref, g_ref, w_ref, wt_ref, o_ref, ot_ref):
    hn = _rms_rows(x_ref[...], g_ref[...]).astype(BF16)
    o_ref[...] = jnp.dot(hn, w_ref[...], preferred_element_type=F32).astype(o_ref.dtype)
    ot_ref[...] = lax.dot_general(wt_ref[...], hn, (((1,), (1,)), ((), ())),
                                  preferred_element_type=F32).astype(ot_ref.dtype)


def _norm_proj(x, g, w, wt, tm=1024):
    t, d = x.shape
    n = w.shape[1]
    nt = wt.shape[0]
    return pl.pallas_call(
        _norm_proj_kernel,
        grid=(t // tm,),
        in_specs=[pl.BlockSpec((tm, d), lambda i: (i, 0)), _resident((1, d)), _resident((d, n)),
                  _resident((nt, d))],
        out_specs=[pl.BlockSpec((tm, n), lambda i: (i, 0)), pl.BlockSpec((nt, tm), lambda i: (0, i))],
        out_shape=[jax.ShapeDtypeStruct((t, n), BF16), jax.ShapeDtypeStruct((nt, t), BF16)],
        compiler_params=_params("parallel"),
        name="norm_proj",
    )(x, g, w, wt)


def _out_proj_kernel(*refs, n_act):
    acts = refs[:n_act]
    w_ref, g_ref, x_ref, o_ref = refs[n_act:]
    k0 = 0
    m = None
    for a in acts:
        k = a.shape[1]
        part = jnp.dot(a[...], w_ref[k0:k0 + k, :], preferred_element_type=F32)
        m = part if m is None else m + part
        k0 += k
    o_ref[...] = x_ref[...] + _rms_rows(m, g_ref[...])


def _out_proj(acts, w, g, x, tm=1024):
    t, d = x.shape
    in_specs = [pl.BlockSpec((tm, a.shape[1]), lambda i: (i, 0)) for a in acts]
    in_specs += [_resident(w.shape), _resident((1, d)), pl.BlockSpec((tm, d), lambda i: (i, 0))]
    return pl.pallas_call(
        functools.partial(_out_proj_kernel, n_act=len(acts)),
        grid=(t // tm,),
        in_specs=in_specs,
        out_specs=pl.BlockSpec((tm, d), lambda i: (i, 0)),
        out_shape=jax.ShapeDtypeStruct((t, d), F32),
        compiler_params=_params("parallel"),
        name="out_proj",
    )(*acts, w, g, x)


def _halo_specs(tm, d, t):
    nb = tm // HALO
    last = t // HALO - 1
    return [
        pl.BlockSpec((HALO, d), lambda i: (jnp.maximum(i * nb - 1, 0), 0)),
        pl.BlockSpec((tm, d), lambda i: (i, 0)),
        pl.BlockSpec((HALO, d), lambda i: (jnp.minimum((i + 1) * nb, last), 0)),
    ]


def _normed_tile_with_halo(xp_ref, x_ref, xn_ref, g_ref, tiles_per_seq):
    i = pl.program_id(0)
    pos = i % tiles_per_seq
    has_prev = (pos != 0).astype(F32)
    has_next = (pos != tiles_per_seq - 1).astype(F32)
    g = g_ref[...]
    top = _rms_rows(xp_ref[...], g) * has_prev
    mid = _rms_rows(x_ref[...], g)
    bot = _rms_rows(xn_ref[...], g) * has_next
    return jnp.concatenate([top, mid, bot], axis=0).astype(BF16), mid.astype(BF16)


def _depthwise_conv_rows(u, w_ref, b_ref, cols, tm):
    n = u.shape[0]
    width = w_ref.shape[0]
    pad = width // 2
    acc = None
    for k in range(width):
        shift = pad - k
        us = u if shift == 0 else pltpu.roll(u, shift % n, 0)
        term = us[HALO:HALO + tm] * w_ref[k:k + 1, cols]
        acc = term if acc is None else acc + term
    return acc + b_ref[:, cols]


def _ffn_kernel(xp_ref, x_ref, xn_ref, gpre_ref, wup_ref, cw_ref, cb_ref, wdn_ref, gpost_ref,
                o_ref, *, tiles_per_seq):
    tm = x_ref.shape[0]
    f = wdn_ref.shape[0]
    hn, _ = _normed_tile_with_halo(xp_ref, x_ref, xn_ref, gpre_ref, tiles_per_seq)
    gcols = slice(0, f)
    vcols = slice(f, 2 * f)
    ug = jnp.dot(hn, wup_ref[:, gcols], preferred_element_type=F32)
    uv = jnp.dot(hn, wup_ref[:, vcols], preferred_element_type=F32)
    gate = _depthwise_conv_rows(ug, cw_ref, cb_ref, gcols, tm)
    val = _depthwise_conv_rows(uv, cw_ref, cb_ref, vcols, tm)
    h = (jax.nn.gelu(gate, approximate=True) * val).astype(BF16)
    m = jnp.dot(h, wdn_ref[...], preferred_element_type=F32)
    o_ref[...] = x_ref[...] + _rms_rows(m, gpost_ref[...])


def _ffn(x, gpre, wup, cw, cb, wdn, gpost, seq, tm=512):
    t, d = x.shape
    f = wdn.shape[0]
    return pl.pallas_call(
        functools.partial(_ffn_kernel, tiles_per_seq=seq // tm),
        grid=(t // tm,),
        in_specs=_halo_specs(tm, d, t) + [
            _resident((1, d)), _resident(wup.shape), _resident(cw.shape), _resident((1, 2 * f)),
            _resident(wdn.shape), _resident((1, d))],
        out_specs=pl.BlockSpec((tm, d), lambda i: (i, 0)),
        out_shape=jax.ShapeDtypeStruct((t, d), F32),
        compiler_params=_params("parallel"),
        name="ffn",
    )(x, x, x, gpre, wup, cw, cb, wdn, gpost)


def _ssd_inproj_kernel(xp_ref, x_ref, xn_ref, g_ref, wz_ref, wx_ref, wdt_ref, cw_ref, cb_ref,
                       z_ref, xbc_ref, dt_ref, *, tiles_per_seq, ncol):
    tm = x_ref.shape[0]
    hn, hm = _normed_tile_with_halo(xp_ref, x_ref, xn_ref, g_ref, tiles_per_seq)
    z_ref[...] = jnp.dot(hm, wz_ref[...], preferred_element_type=F32).astype(z_ref.dtype)
    dt_ref[...] = jnp.dot(hm, wdt_ref[...], preferred_element_type=F32)
    for c in range(0, wx_ref.shape[1], ncol):
        cols = slice(c, c + ncol)
        u = jnp.dot(hn, wx_ref[:, cols], preferred_element_type=F32)
        v = _depthwise_conv_rows(u, cw_ref, cb_ref, cols, tm)
        xbc_ref[:, cols] = (v * jax.nn.sigmoid(v)).astype(xbc_ref.dtype)


def _ssd_inproj(x, g, wz, wx, wdt, cw, cb, seq, tm=512, ncol=SSD_INPROJ_CONV_COLS):
    t, d = x.shape
    nz, nx, ndt = wz.shape[1], wx.shape[1], wdt.shape[1]
    row = lambda n: pl.BlockSpec((tm, n), lambda i: (i, 0))
    return pl.pallas_call(
        functools.partial(_ssd_inproj_kernel, tiles_per_seq=seq // tm, ncol=ncol),
        grid=(t // tm,),
        in_specs=_halo_specs(tm, d, t) + [
            _resident((1, d)), _resident(wz.shape), _resident(wx.shape), _resident(wdt.shape),
            _resident(cw.shape), _resident((1, nx))],
        out_specs=[row(nz), row(nx), row(ndt)],
        out_shape=[jax.ShapeDtypeStruct((t, nz), BF16), jax.ShapeDtypeStruct((t, nx), BF16),
                   jax.ShapeDtypeStruct((t, ndt), F32)],
        compiler_params=_params("parallel"),
        name="ssd_inproj",
    )(x, x, x, g, wz, wx, wdt, cw, cb)


def _chunk_of(p, c, nc):
    return p * c + (1 - p) * (nc - 1 - c)


def _first_half_mask(shape):
    return lax.broadcasted_iota(jnp.int32, shape, len(shape) - 1) < HEAD_DIM


def _ret_kernel(q_ref, k_ref, v_ref, g_ref, cos_ref, sin_ref, lgf_ref, lgb_ref, gn_ref, gm_ref,
                o_ref, hf, hb, hbs, dm, ptab, *, nc):
    p = pl.program_id(1)
    c = pl.program_id(2)
    L = k_ref.shape[0]
    npair = k_ref.shape[1] // LANES
    lgf = lgf_ref[...]
    lgb = lgb_ref[...]
    first = _first_half_mask((1, LANES))
    swap_up = (lax.broadcasted_iota(jnp.int32, (1, LANES), 1) % HEAD_DIM) < HEAD_DIM // 2
    ri = lax.broadcasted_iota(jnp.int32, (LANES, LANES), 0) < HEAD_DIM
    ci = lax.broadcasted_iota(jnp.int32, (LANES, LANES), 1) < HEAD_DIM
    same_head = ri == ci
    cos_f = cos_ref[...]
    sin_s = sin_ref[...]
    scale = HEAD_DIM ** -0.5
    TAIL_B, DEC_F, DEC_B, TAIL_F = range(4)

    def rot(x):
        sw = jnp.where(swap_up, pltpu.roll(x, LANES - HEAD_DIM // 2, 1),
                       pltpu.roll(x, HEAD_DIM // 2, 1))
        return x * cos_f + sw * sin_s

    @pl.when((p == 0) & (c == 0))
    def _init():
        hf[...] = jnp.zeros_like(hf)
        hb[...] = jnp.zeros_like(hb)
        d = (lax.broadcasted_iota(jnp.int32, (L, L), 0)
             - lax.broadcasted_iota(jnp.int32, (L, L), 1)).astype(F32)
        for h in range(2 * npair):
            lf = lgf[:, HEAD_DIM * h:HEAD_DIM * h + 1]
            lb = lgb[:, HEAD_DIM * h:HEAD_DIM * h + 1]
            dm[h] = jnp.exp(jnp.where(d >= 0, d * lf, -d * lb))
        jj = lax.broadcasted_iota(jnp.int32, (L, 1), 0).astype(F32)
        ptab[TAIL_B] = jnp.exp(jj * lgb)
        ptab[DEC_F] = jnp.exp((jj + 1.0) * lgf)
        ptab[DEC_B] = jnp.exp((L - jj) * lgb)
        ptab[TAIL_F] = jnp.exp((L - 1.0 - jj) * lgf)

    @pl.when(p == 0)
    def _backward_states():
        ch = nc - 1 - c
        k = k_ref[...].astype(F32)
        decay_b = jnp.exp(L * lgb)
        for pr in range(npair):
            sl = slice(LANES * pr, LANES * (pr + 1))
            kp = rot(k[:, sl]) * scale
            kt = (kp * ptab[TAIL_B, :, sl]).T.astype(BF16)
            hbs[ch, pr] = hb[pr].astype(BF16)
            upd = jnp.dot(kt, v_ref[:, sl], preferred_element_type=F32)
            hb[pr] = jnp.where(same_head, hb[pr] * decay_b[:, sl] + upd, 0.0)

    @pl.when(p == 1)
    def _forward():
        q = q_ref[...].astype(F32)
        k = k_ref[...].astype(F32)
        decay_f = jnp.exp(L * lgf)
        ys = []
        for pr in range(npair):
            sl = slice(LANES * pr, LANES * (pr + 1))
            qp = rot(q[:, sl])
            kp = rot(k[:, sl]) * scale
            kb = kp.astype(BF16)
            vp = v_ref[:, sl]
            res = []
            for hh in range(2):
                keep = first if hh == 0 else jnp.logical_not(first)
                qm = jnp.where(keep, qp, 0.0).astype(BF16)
                s = lax.dot_general(qm, kb, (((1,), (1,)), ((), ())), preferred_element_type=F32)
                a = (s * dm[2 * pr + hh]).astype(BF16)
                res.append(jnp.dot(a, vp, preferred_element_type=F32))
            y = jnp.where(first, res[0], res[1])
            y = y + jnp.dot((qp * ptab[DEC_F, :, sl]).astype(BF16), hf[pr].astype(BF16),
                            preferred_element_type=F32)
            y = y + jnp.dot((qp * ptab[DEC_B, :, sl]).astype(BF16), hbs[c, pr],
                            preferred_element_type=F32)
            kt = (kp * ptab[TAIL_F, :, sl]).T.astype(BF16)
            upd = jnp.dot(kt, vp, preferred_element_type=F32)
            hf[pr] = jnp.where(same_head, hf[pr] * decay_f[:, sl] + upd, 0.0)
            ys.append(y)
        y = jnp.concatenate(ys, axis=1)
        gm = gm_ref[...]
        mu = _split2_dot(y, gm)
        d = y - mu
        var = _split2_dot(d * d, gm)
        yn = d * lax.rsqrt(var + EPS) * gn_ref[...]
        g = g_ref[...].astype(F32)
        o_ref[...] = (g * jax.nn.sigmoid(g) * yn).astype(o_ref.dtype)


def _retention(proj, cos_t, sin_t, lgf, lgb, gn, gmat, batch, seq, width):
    L = RET_CHUNK
    nc = seq // L
    nw = width // LANES
    t = proj.shape[0]

    def kv_map(col):
        return lambda b, p, c: (b * nc + _chunk_of(p, c, nc), col)

    def fwd_map(col):
        return lambda b, p, c: (b * nc + p * c, col)

    tab_map = lambda b, p, c: (_chunk_of(p, c, nc), 0)
    return pl.pallas_call(
        functools.partial(_ret_kernel, nc=nc),
        grid=(batch, 2, nc),
        in_specs=[pl.BlockSpec((L, width), fwd_map(0)), pl.BlockSpec((L, width), kv_map(1)),
                  pl.BlockSpec((L, width), kv_map(2)), pl.BlockSpec((L, width), fwd_map(3)),
                  pl.BlockSpec((L, LANES), tab_map), pl.BlockSpec((L, LANES), tab_map),
                  _resident((1, width)), _resident((1, width)), _resident((1, width)),
                  _resident((width, width))],
        out_specs=pl.BlockSpec((L, width), fwd_map(0)),
        out_shape=jax.ShapeDtypeStruct((t, width), BF16),
        scratch_shapes=[pltpu.VMEM((nw, LANES, LANES), F32), pltpu.VMEM((nw, LANES, LANES), F32),
                        pltpu.VMEM((nc, nw, LANES, LANES), BF16), pltpu.VMEM((2 * nw, L, L), F32),
                        pltpu.VMEM((4, L, width), F32)],
        compiler_params=_params("arbitrary", "arbitrary", "arbitrary"),
        name="retention",
    )(proj, proj, proj, proj, cos_t, sin_t, lgf, lgb, gn, gmat)


def _na_kernel(q_ref, kp_ref, kc_ref, kn_ref, vp_ref, vc_ref, vn_ref, bias_ref, o_ref,
               kbuf, vtb, *, rows_total):
    i = pl.program_id(1)
    halo = kp_ref.shape[0]
    main = kc_ref.shape[0]
    span = main + 2 * halo
    nkeys = NA_WIN_R * GRID_W
    npair = o_ref.shape[1] // LANES
    kbuf[0:halo] = kp_ref[...]
    kbuf[halo:halo + main] = kc_ref[...]
    kbuf[halo + main:] = kn_ref[...]
    vtb[0, :, 0:halo] = vp_ref[...]
    vtb[0, :, halo:halo + main] = vc_ref[...]
    vtb[0, :, halo + main:] = vn_ref[...]
    vtb[1] = pltpu.roll(vtb[0], span - GRID_W, 1)
    first = _first_half_mask((1, LANES))
    scale = HEAD_DIM ** -0.5
    buf_row0 = i * NA_ROWS_PER_STEP - NA_WIN_R // 2

    def row_body(r, carry):
        gr = i * NA_ROWS_PER_STEP + r
        r0 = jnp.clip(gr - NA_WIN_R // 2, 0, rows_total - NA_WIN_R)
        boff = pl.multiple_of((NA_WIN_R - 1 - (gr - r0)) * GRID_W, GRID_W)
        rel = r0 - buf_row0
        odd = rel % 2
        off = pl.multiple_of(rel * GRID_W, GRID_W)
        base = pl.multiple_of((rel - odd) * GRID_W, LANES)
        qoff = pl.multiple_of(r * GRID_W, GRID_W)
        qrow = q_ref[pl.ds(qoff, GRID_W), :].astype(F32) * scale
        outs = []
        for pr in range(npair):
            sl = slice(LANES * pr, LANES * (pr + 1))
            qp = qrow[:, sl]
            qm = jnp.concatenate([jnp.where(first, qp, 0.0), jnp.where(first, 0.0, qp)],
                                 axis=0).astype(BF16)
            kwin = kbuf[pl.ds(off, nkeys), sl]
            st = lax.dot_general(kwin, qm, (((1,), (1,)), ((), ())), preferred_element_type=F32)
            st = st + bias_ref[pr, pl.ds(boff, nkeys), :]
            m = jnp.max(st, axis=0, keepdims=True)
            e = jnp.exp(st - m)
            l = jnp.sum(e, axis=0, keepdims=True)
            vtw = vtb[odd, sl, pl.ds(base, nkeys)]
            ot = jnp.dot(vtw, e.astype(BF16), preferred_element_type=F32) / l
            o = ot.T
            outs.append(jnp.where(first, o[0:GRID_W], o[GRID_W:2 * GRID_W]))
        o_ref[pl.ds(qoff, GRID_W), :] = jnp.concatenate(outs, axis=1).astype(o_ref.dtype)
        return carry

    lax.fori_loop(0, NA_ROWS_PER_STEP, row_body, 0, unroll=NA_ROW_UNROLL)


def _neighborhood_attention(proj, vt, bias, batch, seq, width, q_col, k_col):
    assert 2 * GRID_W == LANES
    rows = seq // GRID_W
    nstep = rows // NA_ROWS_PER_STEP
    main = NA_ROWS_PER_STEP * GRID_W
    halo = (NA_WIN_R // 2) * GRID_W
    per = main // halo
    nhalo = seq // halo
    t = proj.shape[0]

    main_i = lambda b, i: b * nstep + i
    prev_i = lambda b, i: b * nhalo + jnp.maximum(i * per - 1, 0)
    next_i = lambda b, i: b * nhalo + jnp.minimum((i + 1) * per, nhalo - 1)
    rowblk = lambda f, col: (lambda b, i: (f(b, i), col))
    colblk = lambda f: (lambda b, i: (0, f(b, i)))

    return pl.pallas_call(
        functools.partial(_na_kernel, rows_total=rows),
        grid=(batch, nstep),
        in_specs=[pl.BlockSpec((main, width), rowblk(main_i, q_col)),
                  pl.BlockSpec((halo, width), rowblk(prev_i, k_col)),
                  pl.BlockSpec((main, width), rowblk(main_i, k_col)),
                  pl.BlockSpec((halo, width), rowblk(next_i, k_col)),
                  pl.BlockSpec((width, halo), colblk(prev_i)),
                  pl.BlockSpec((width, main), colblk(main_i)),
                  pl.BlockSpec((width, halo), colblk(next_i)),
                  _resident(bias.shape)],
        out_specs=pl.BlockSpec((main, width), rowblk(main_i, 0)),
        out_shape=jax.ShapeDtypeStruct((t, width), BF16),
        scratch_shapes=[pltpu.VMEM((main + 2 * halo, width), BF16),
                        pltpu.VMEM((2, width, main + 2 * halo), BF16)],
        compiler_params=_params("parallel", "parallel"),
        name="neighborhood_attention",
    )(proj, proj, proj, proj, vt, vt, vt, bias)


def _na_bias_table(rpb):
    qc = jnp.arange(GRID_W)
    kc = jnp.arange(GRID_W)
    cstart = jnp.clip(qc - NA_WIN_C // 2, 0, GRID_W - NA_WIN_C)
    valid = (kc[None, :] >= cstart[:, None]) & (kc[None, :] < cstart[:, None] + NA_WIN_C)
    dc = jnp.clip(kc[None, :] - qc[:, None], -(NA_WIN_C - 1), NA_WIN_C - 1) + (NA_WIN_C - 1)
    onehot = (dc[None] == jnp.arange(2 * NA_WIN_C - 1)[:, None, None]).astype(F32)
    toep = jnp.einsum('hrd,dqk->hrqk', rpb.astype(F32), onehot, precision=lax.Precision.HIGHEST)
    toep = jnp.where(valid[None, None], toep, NEG_BIG)
    nh, ndr = rpb.shape[0], rpb.shape[1]
    tab = toep.reshape(nh // 2, 2, ndr, GRID_W, GRID_W)
    tab = jnp.transpose(tab, (0, 2, 4, 1, 3))
    return tab.reshape(nh // 2, ndr * GRID_W, 2 * GRID_W)


def _ssd_decay_kernel(dt_ref, dtb_ref, alog_ref, cs_ref, cse_ref, rf_ref, rb_ref):
    L = SSD_CHUNK
    tri = (lax.broadcasted_iota(jnp.int32, (L, L), 0)
           >= lax.broadcasted_iota(jnp.int32, (L, L), 1)).astype(BF16)
    raw = dt_ref[...] + dtb_ref[...]
    dt = jnp.maximum(raw, 0.0) + jnp.log1p(jnp.exp(-jnp.abs(raw)))
    a = dt * (-jnp.exp(alog_ref[...])) * LOG2_E
    ldt = jnp.log2(dt)
    for k in range(dt_ref.shape[0] // L):
        rows = slice(k * L, (k + 1) * L)
        cs = _split3_dot(tri, a[rows])
        cse = cs - a[rows]
        cs_ref[rows, :] = cs
        cse_ref[rows, :] = cse
        rf_ref[:, rows] = (cs - ldt[rows]).T
        rb_ref[:, rows] = (cse + ldt[rows]).T


def _ssd_decay(dt, dtb, alog, chunks_per_step=8):
    L = SSD_CHUNK * chunks_per_step
    t = dt.shape[0]
    col = pl.BlockSpec((L, LANES), lambda i: (i, 0))
    rowm = pl.BlockSpec((LANES, L), lambda i: (0, i))
    return pl.pallas_call(
        _ssd_decay_kernel,
        grid=(t // L,),
        in_specs=[col, _resident((1, LANES)), _resident((1, LANES))],
        out_specs=[col, col, rowm, rowm],
        out_shape=[jax.ShapeDtypeStruct((t, LANES), F32), jax.ShapeDtypeStruct((t, LANES), F32),
                   jax.ShapeDtypeStruct((LANES, t), F32), jax.ShapeDtypeStruct((LANES, t), F32)],
        compiler_params=_params("parallel"),
        name="ssd_decay",
    )(dt, dtb, alog)


def _ssd_kernel(x_ref, b_ref, c_ref, z_ref, cs_ref, cse_ref, rf_ref, rb_ref, dsk_ref, ng_ref,
                o_ref, hf, hb, hbs, *, nc, hpg, groups):
    p = pl.program_id(1)
    c = pl.program_id(2)
    L = SSD_CHUNK
    cps = x_ref.shape[0] // L
    assert L == SSD_STATE == LANES
    gw = hpg * SSD_HEADDIM
    npair = gw // LANES
    first = _first_half_mask((1, LANES))
    rj = lax.broadcasted_iota(jnp.int32, (L, L), 0)
    rl = lax.broadcasted_iota(jnp.int32, (L, L), 1)
    lower = rj >= rl

    @pl.when((p == 0) & (c == 0))
    def _init():
        hf[...] = jnp.zeros_like(hf)
        hb[...] = jnp.zeros_like(hb)

    def pair_scale(vals):
        return jnp.where(first, vals[0], vals[1])

    def row(m, lane):
        return m[lane:lane + 1, :]

    def col_bcast(m, lane):
        return jnp.broadcast_to(m[:, lane:lane + 1], (L, LANES))

    def backward_chunk(rows, ch):
        cs = cs_ref[rows, :]
        rows_b = rb_ref[:, rows]
        tot = cs[L - 1:L, :]
        for g in range(groups):
            bt = b_ref[rows, SSD_STATE * g:SSD_STATE * (g + 1)].astype(F32).T
            for pr in range(npair):
                idx = g * npair + pr
                xp = x_ref[rows, LANES * idx:LANES * (idx + 1)]
                upd, dec = [], []
                for hh in range(2):
                    lb = 2 * hpg * g + hpg + 2 * pr + hh
                    srow = jnp.exp2(row(rows_b, lb))
                    upd.append(jnp.dot((bt * srow).astype(BF16), xp, preferred_element_type=F32))
                    dec.append(jnp.exp2(tot[:, lb:lb + 1]))
                hbs[ch, idx] = hb[idx].astype(BF16)
                hb[idx] = hb[idx] * pair_scale(dec) + pair_scale(upd)

    def forward_chunk(rows, ch):
        cs = cs_ref[rows, :]
        cse = cse_ref[rows, :]
        rows_f = rf_ref[:, rows]
        rows_b = rb_ref[:, rows]
        tot = cs[L - 1:L, :]
        for g in range(groups):
            bm = b_ref[rows, SSD_STATE * g:SSD_STATE * (g + 1)]
            cm = c_ref[rows, SSD_STATE * g:SSD_STATE * (g + 1)]
            bt = bm.astype(F32).T
            cb = lax.dot_general(cm, bm, (((1,), (1,)), ((), ())), preferred_element_type=F32)
            cf32 = cm.astype(F32)
            ys = []
            for pr in range(npair):
                idx = g * npair + pr
                sl = slice(LANES * idx, LANES * (idx + 1))
                xp = x_ref[rows, sl]
                hf_b = hf[idx].astype(BF16)
                hb_b = hbs[ch, idx]
                res, upd, dec = [], [], []
                for hh in range(2):
                    lf = 2 * hpg * g + 2 * pr + hh
                    lb = lf + hpg
                    cfc = col_bcast(cs, lf)
                    cbc = col_bcast(cse, lb)
                    e = jnp.where(lower, cfc - row(rows_f, lf), row(rows_b, lb) - cbc)
                    m = (cb * jnp.exp2(e)).astype(BF16)
                    c_f = (cf32 * jnp.exp2(cfc)).astype(BF16)
                    c_b = (cf32 * jnp.exp2(tot[:, lb:lb + 1] - cbc)).astype(BF16)
                    res.append(jnp.dot(m, xp, preferred_element_type=F32)
                               + jnp.dot(c_f, hf_b, preferred_element_type=F32)
                               + jnp.dot(c_b, hb_b, preferred_element_type=F32))
                    srow = jnp.exp2(tot[:, lf:lf + 1] - row(rows_f, lf))
                    upd.append(jnp.dot((bt * srow).astype(BF16), xp, preferred_element_type=F32))
                    dec.append(jnp.exp2(tot[:, lf:lf + 1]))
                hf[idx] = hf[idx] * pair_scale(dec) + pair_scale(upd)
                y = pair_scale(res) + xp.astype(F32) * dsk_ref[:, sl]
                z = z_ref[rows, sl].astype(F32)
                ys.append(y * (z * jax.nn.sigmoid(z)))
            y = jnp.concatenate(ys, axis=1)
            gsl = slice(gw * g, gw * (g + 1))
            o_ref[rows, gsl] = _rms_rows(y, ng_ref[:, gsl]).astype(o_ref.dtype)

    @pl.when(p == 0)
    def _backward_states():
        blk = nc // cps - 1 - c
        for sub in reversed(range(cps)):
            backward_chunk(slice(sub * L, (sub + 1) * L), blk * cps + sub)

    @pl.when(p == 1)
    def _forward():
        for sub in range(cps):
            forward_chunk(slice(sub * L, (sub + 1) * L), c * cps + sub)


def _ssd_scan(xbc, z, decay, dsk, ng, batch, seq, inner, hpg):
    cs, cse, rows_f, rows_b = decay
    L = SSD_CHUNK
    nc = seq // L
    groups = inner // (hpg * SSD_HEADDIM)
    npairs = inner // LANES
    bc_w = groups * SSD_STATE
    t = xbc.shape[0]
    b_col = inner // bc_w

    nb = nc // SSD_CHUNKS_PER_STEP
    lb = L * SSD_CHUNKS_PER_STEP

    def any_map(col):
        return lambda b, p, c: (b * nb + _chunk_of(p, c, nb), col)

    def fwd_map(col):
        return lambda b, p, c: (b * nb + p * c, col)

    any_t = lambda b, p, c: (0, b * nb + _chunk_of(p, c, nb))
    fwd_t = lambda b, p, c: (0, b * nb + p * c)
    return pl.pallas_call(
        functools.partial(_ssd_kernel, nc=nc, hpg=hpg, groups=groups),
        grid=(batch, 2, nb),
        in_specs=[pl.BlockSpec((lb, inner), any_map(0)),
                  pl.BlockSpec((lb, bc_w), any_map(b_col)),
                  pl.BlockSpec((lb, bc_w), fwd_map(b_col + 1)),
                  pl.BlockSpec((lb, inner), fwd_map(0)),
                  pl.BlockSpec((lb, LANES), any_map(0)), pl.BlockSpec((lb, LANES), fwd_map(0)),
                  pl.BlockSpec((LANES, lb), fwd_t), pl.BlockSpec((LANES, lb), any_t),
                  _resident((1, inner)), _resident((1, inner))],
        out_specs=pl.BlockSpec((lb, inner), fwd_map(0)),
        out_shape=jax.ShapeDtypeStruct((t, inner), BF16),
        scratch_shapes=[pltpu.VMEM((npairs, SSD_STATE, LANES), F32),
                        pltpu.VMEM((npairs, SSD_STATE, LANES), F32),
                        pltpu.VMEM((nc, npairs, SSD_STATE, LANES), BF16)],
        compiler_params=_params("arbitrary", "arbitrary", "arbitrary"),
        name="ssd_scan",
    )(xbc, xbc, xbc, z, cs, cse, rows_f, rows_b, dsk, ng)


def _rotary_tables(seq):
    half = HEAD_DIM // 2
    inv = 1.0 / (ROPE_BASE ** (jnp.arange(half, dtype=F32) / half))
    ang = jnp.arange(seq).astype(F32)[:, None] * inv[None, :]
    cos = jnp.cos(ang)
    sin = jnp.sin(ang)
    reps = LANES // HEAD_DIM
    cos_t = jnp.tile(jnp.concatenate([cos, cos], axis=1), (1, reps))
    sin_t = jnp.tile(jnp.concatenate([-sin, sin], axis=1), (1, reps))
    return cos_t, sin_t


def _group_dt_lanes(v, hpg):
    lead = v.shape[:-2]
    heads = v.shape[-1]
    groups = heads // hpg
    assert 2 * heads <= LANES
    vg = v.reshape(*lead, 2, groups, hpg)
    vg = jnp.moveaxis(vg, -3, -2).reshape(*lead, 2 * heads)
    pad = [(0, 0)] * (vg.ndim - 1) + [(0, LANES - 2 * heads)]
    return jnp.pad(vg, pad)


def kernel(x, norm_mix_pre, norm_mix_post, norm_ffn_pre, norm_ffn_post, ab_w_in, ab_ret_decay_logit, ab_ret_gn_g, ab_na_rpb, ab_w_out, c_w_in, c_conv_w, c_conv_b, c_dt_bias, c_a_log, c_d_skip, c_norm_g, c_w_out, ffn_w_up, ffn_conv_w, ffn_conv_b, ffn_w_down):
    batch, seq, d = x.shape
    depth = norm_mix_pre.shape[0]
    t = batch * seq
    ret_w = RET_HEADS * HEAD_DIM
    na_w = NA_HEADS * HEAD_DIM
    inner = c_w_out.shape[1]
    heads = inner // SSD_HEADDIM
    hpg = heads // SSD_GROUPS
    xbc_w = inner + 2 * SSD_GROUPS * SSD_STATE
    assert ret_w == na_w and ab_w_in.shape[2] == 4 * ret_w + 3 * na_w

    h = x.reshape(t, d).astype(F32)
    cos_t, sin_t = _rotary_tables(seq)
    gmat = jnp.kron(jnp.eye(RET_HEADS, dtype=F32),
                    jnp.full((HEAD_DIM, HEAD_DIM), 1.0 / HEAD_DIM, F32)).astype(BF16)
    row = lambda v: v.reshape(1, -1).astype(F32)

    for layer in range(depth):
        i = layer // 2
        if layer % 2 == 0:
            n_tok = 4 * ret_w + 2 * na_w
            proj, na_vt = _norm_proj(h, row(norm_mix_pre[layer]),
                                     ab_w_in[i][:, :n_tok].astype(BF16),
                                     ab_w_in[i][:, n_tok:].T.astype(BF16))
            lg = -jax.nn.softplus(-ab_ret_decay_logit[i].astype(F32))
            lgf = jnp.repeat(lg[0], HEAD_DIM).reshape(1, ret_w)
            lgb = jnp.repeat(lg[1], HEAD_DIM).reshape(1, ret_w)
            ret = _retention(proj, cos_t, sin_t, lgf, lgb, row(ab_ret_gn_g[i]), gmat,
                             batch, seq, ret_w)
            na = _neighborhood_attention(proj, na_vt, _na_bias_table(ab_na_rpb[i]), batch, seq, na_w,
                                         4 * ret_w // na_w, 4 * ret_w // na_w + 1)
            h = _out_proj([ret, na], ab_w_out[i].astype(BF16), row(norm_mix_post[layer]), h)
        else:
            w_in = c_w_in[i]
            wz = w_in[:, :inner].astype(BF16)
            wx = w_in[:, inner:inner + xbc_w].astype(BF16)
            wdt = _group_dt_lanes(w_in[:, inner + xbc_w:].reshape(d, 2, heads), hpg).astype(BF16)
            z, xbc, dtr = _ssd_inproj(h, row(norm_mix_pre[layer]), wz, wx, wdt,
                                      c_conv_w[i].astype(F32), row(c_conv_b[i]), seq)
            dtb = _group_dt_lanes(c_dt_bias[i].astype(F32), hpg).reshape(1, -1)
            alog = _group_dt_lanes(c_a_log[i].astype(F32), hpg).reshape(1, -1)
            dsk = jnp.repeat(c_d_skip[i].astype(F32), SSD_HEADDIM).reshape(1, inner)
            y = _ssd_scan(xbc, z, _ssd_decay(dtr, dtb, alog), dsk, row(c_norm_g[i]),
                          batch, seq, inner, hpg)
            h = _out_proj([y], c_w_out[i].astype(BF16), row(norm_mix_post[layer]), h)
        h = _ffn(h, row(norm_ffn_pre[layer]), ffn_w_up[layer].astype(BF16),
                 ffn_conv_w[layer].astype(F32), row(ffn_conv_b[layer]),
                 ffn_w_down[layer].astype(BF16), row(norm_ffn_post[layer]), seq)
    return h.reshape(batch, seq, d).astype(x.dtype)
```

```python
import functools
import math

import jax
import jax.numpy as jnp
from jax import lax
from jax.experimental import pallas as pl
from jax.experimental.pallas import tpu as pltpu

F32 = jnp.float32
BF16 = jnp.bfloat16

EPS = 1e-6
GRID_W = 64
ROPE_BASE = 10000.0
RET_HEADS = 8
NA_HEADS = 8
NA_WIN_R = 8
NA_WIN_C = 16
SSD_HEADDIM = 64
SSD_GROUPS = 4
SSD_STATE = 128
HEAD_DIM = 64
LANES = 128
HALO = 8
RET_CHUNK = 256
RET_CHUNKS_PER_STEP = 4
SSD_CHUNK = 128
SSD_CHUNKS_PER_STEP = 4
NA_ROWS_PER_STEP = 32
NA_ROW_UNROLL = 8
SSD_INPROJ_CONV_COLS = 1024
NEG_BIG = -1e30
LOG2_E = 1.0 / math.log(2.0)
VMEM_LIMIT = 56 * 1024 * 1024


def _params(*sem):
    return pltpu.CompilerParams(dimension_semantics=sem, vmem_limit_bytes=VMEM_LIMIT)


def _resident(shape):
    nd = len(shape)
    return pl.BlockSpec(shape, lambda *_: (0,) * nd, pipeline_mode=pl.Buffered(1))


def _rms_rows(x, g):
    return x * lax.rsqrt(jnp.mean(x * x, axis=-1, keepdims=True) + EPS) * g


def _split3_dot(tri, a):
    hi = a.astype(BF16)
    r1 = a - hi.astype(F32)
    mid = r1.astype(BF16)
    lo = (r1 - mid.astype(F32)).astype(BF16)
    return (jnp.dot(tri, hi, preferred_element_type=F32)
            + jnp.dot(tri, mid, preferred_element_type=F32)
            + jnp.dot(tri, lo, preferred_element_type=F32))


def _split2_dot(a, m):
    hi = a.astype(BF16)
    lo = (a - hi.astype(F32)).astype(BF16)
    return jnp.dot(hi, m, preferred_element_type=F32) + jnp.dot(lo, m, preferred_element_type=F32)


def _norm_proj_kernel(x_ref, g_ref, w_ref, wt_ref, o_ref, ot_ref):
    hn = _rms_rows(x_ref[...], g_ref[...]).astype(BF16)
    o_ref[...] = jnp.dot(hn, w_ref[...], preferred_element_type=F32).astype(o_ref.dtype)
    ot_ref[...] = lax.dot_general(wt_ref[...], hn, (((1,), (1,)), ((), ())),
                                  preferred_element_type=F32).astype(ot_ref.dtype)


def _norm_proj(x, g, w, wt, tm=1024):
    t, d = x.shape
    n = w.shape[1]
    nt = wt.shape[0]
    return pl.pallas_call(
        _norm_proj_kernel,
        grid=(t // tm,),
        in_specs=[pl.BlockSpec((tm, d), lambda i: (i, 0)), _resident((1, d)), _resident((d, n)),
                  _resident((nt, d))],
        out_specs=[pl.BlockSpec((tm, n), lambda i: (i, 0)), pl.BlockSpec((nt, tm), lambda i: (0, i))],
        out_shape=[jax.ShapeDtypeStruct((t, n), BF16), jax.ShapeDtypeStruct((nt, t), BF16)],
        compiler_params=_params("parallel"),
        name="norm_proj",
    )(x, g, w, wt)


def _out_proj_kernel(*refs, n_act):
    acts = refs[:n_act]
    w_ref, g_ref, x_ref, o_ref = refs[n_act:]
    k0 = 0
    m = None
    for a in acts:
        k = a.shape[1]
        part = jnp.dot(a[...], w_ref[k0:k0 + k, :], preferred_element_type=F32)
        m = part if m is None else m + part
        k0 += k
    o_ref[...] = x_ref[...] + _rms_rows(m, g_ref[...])


def _out_proj(acts, w, g, x, tm=1024):
    t, d = x.shape
    in_specs = [pl.BlockSpec((tm, a.shape[1]), lambda i: (i, 0)) for a in acts]
    in_specs += [_resident(w.shape), _resident((1, d)), pl.BlockSpec((tm, d), lambda i: (i, 0))]
    return pl.pallas_call(
        functools.partial(_out_proj_kernel, n_act=len(acts)),
        grid=(t // tm,),
        in_specs=in_specs,
        out_specs=pl.BlockSpec((tm, d), lambda i: (i, 0)),
        out_shape=jax.ShapeDtypeStruct((t, d), F32),
        compiler_params=_params("parallel"),
        name="out_proj",
    )(*acts, w, g, x)


def _halo_specs(tm, d, t):
    nb = tm // HALO
    last = t // HALO - 1
    return [
        pl.BlockSpec((HALO, d), lambda i: (jnp.maximum(i * nb - 1, 0), 0)),
        pl.BlockSpec((tm, d), lambda i: (i, 0)),
        pl.BlockSpec((HALO, d), lambda i: (jnp.minimum((i + 1) * nb, last), 0)),
    ]


def _normed_tile_with_halo(xp_ref, x_ref, xn_ref, g_ref, tiles_per_seq):
    i = pl.program_id(0)
    pos = i % tiles_per_seq
    has_prev = (pos != 0).astype(F32)
    has_next = (pos != tiles_per_seq - 1).astype(F32)
    g = g_ref[...]
    top = _rms_rows(xp_ref[...], g) * has_prev
    mid = _rms_rows(x_ref[...], g)
    bot = _rms_rows(xn_ref[...], g) * has_next
    return jnp.concatenate([top, mid, bot], axis=0).astype(BF16), mid.astype(BF16)


def _depthwise_conv_rows(u, w_ref, b_ref, cols, tm):
    n = u.shape[0]
    width = w_ref.shape[0]
    pad = width // 2
    acc = None
    for k in range(width):
        shift = pad - k
        us = u if shift == 0 else pltpu.roll(u, shift % n, 0)
        term = us[HALO:HALO + tm] * w_ref[k:k + 1, cols]
        acc = term if acc is None else acc + term
    return acc + b_ref[:, cols]


def _ffn_kernel(xp_ref, x_ref, xn_ref, gpre_ref, wup_ref, cw_ref, cb_ref, wdn_ref, gpost_ref,
                o_ref, *, tiles_per_seq):
    tm = x_ref.shape[0]
    f = wdn_ref.shape[0]
    hn, _ = _normed_tile_with_halo(xp_ref, x_ref, xn_ref, gpre_ref, tiles_per_seq)
    gcols = slice(0, f)
    vcols = slice(f, 2 * f)
    ug = jnp.dot(hn, wup_ref[:, gcols], preferred_element_type=F32)
    uv = jnp.dot(hn, wup_ref[:, vcols], preferred_element_type=F32)
    gate = _depthwise_conv_rows(ug, cw_ref, cb_ref, gcols, tm)
    val = _depthwise_conv_rows(uv, cw_ref, cb_ref, vcols, tm)
    h = (jax.nn.gelu(gate, approximate=True) * val).astype(BF16)
    m = jnp.dot(h, wdn_ref[...], preferred_element_type=F32)
    o_ref[...] = x_ref[...] + _rms_rows(m, gpost_ref[...])


def _ffn(x, gpre, wup, cw, cb, wdn, gpost, seq, tm=512):
    t, d = x.shape
    f = wdn.shape[0]
    return pl.pallas_call(
        functools.partial(_ffn_kernel, tiles_per_seq=seq // tm),
        grid=(t // tm,),
        in_specs=_halo_specs(tm, d, t) + [
            _resident((1, d)), _resident(wup.shape), _resident(cw.shape), _resident((1, 2 * f)),
            _resident(wdn.shape), _resident((1, d))],
        out_specs=pl.BlockSpec((tm, d), lambda i: (i, 0)),
        out_shape=jax.ShapeDtypeStruct((t, d), F32),
        compiler_params=_params("parallel"),
        name="ffn",
    )(x, x, x, gpre, wup, cw, cb, wdn, gpost)


def _ssd_inproj_kernel(xp_ref, x_ref, xn_ref, g_ref, wz_ref, wx_ref, wdt_ref, cw_ref, cb_ref,
                       z_ref, xbc_ref, dt_ref, *, tiles_per_seq, ncol):
    tm = x_ref.shape[0]
    hn, hm = _normed_tile_with_halo(xp_ref, x_ref, xn_ref, g_ref, tiles_per_seq)
    z_ref[...] = jnp.dot(hm, wz_ref[...], preferred_element_type=F32).astype(z_ref.dtype)
    dt_ref[...] = jnp.dot(hm, wdt_ref[...], preferred_element_type=F32)
    for c in range(0, wx_ref.shape[1], ncol):
        cols = slice(c, c + ncol)
        u = jnp.dot(hn, wx_ref[:, cols], preferred_element_type=F32)
        v = _depthwise_conv_rows(u, cw_ref, cb_ref, cols, tm)
        xbc_ref[:, cols] = (v * jax.nn.sigmoid(v)).astype(xbc_ref.dtype)


def _ssd_inproj(x, g, wz, wx, wdt, cw, cb, seq, tm=512, ncol=SSD_INPROJ_CONV_COLS):
    t, d = x.shape
    nz, nx, ndt = wz.shape[1], wx.shape[1], wdt.shape[1]
    row = lambda n: pl.BlockSpec((tm, n), lambda i: (i, 0))
    return pl.pallas_call(
        functools.partial(_ssd_inproj_kernel, tiles_per_seq=seq // tm, ncol=ncol),
        grid=(t // tm,),
        in_specs=_halo_specs(tm, d, t) + [
            _resident((1, d)), _resident(wz.shape), _resident(wx.shape), _resident(wdt.shape),
            _resident(cw.shape), _resident((1, nx))],
        out_specs=[row(nz), row(nx), row(ndt)],
        out_shape=[jax.ShapeDtypeStruct((t, nz), BF16), jax.ShapeDtypeStruct((t, nx), BF16),
                   jax.ShapeDtypeStruct((t, ndt), F32)],
        compiler_params=_params("parallel"),
        name="ssd_inproj",
    )(x, x, x, g, wz, wx, wdt, cw, cb)


def _chunk_of(p, c, nc):
    return p * c + (1 - p) * (nc - 1 - c)


def _first_half_mask(shape):
    return lax.broadcasted_iota(jnp.int32, shape, len(shape) - 1) < HEAD_DIM


def _ret_kernel(q_ref, k_ref, v_ref, g_ref, cos_ref, sin_ref, lgf_ref, lgb_ref, gn_ref, gm_ref,
                o_ref, hf, hb, hbs, dm, ptab, *, nc):
    p = pl.program_id(1)
    c = pl.program_id(2)
    L = RET_CHUNK
    cps = k_ref.shape[0] // L
    npair = k_ref.shape[1] // LANES
    lgf = lgf_ref[...]
    lgb = lgb_ref[...]
    first = _first_half_mask((1, LANES))
    swap_up = (lax.broadcasted_iota(jnp.int32, (1, LANES), 1) % HEAD_DIM) < HEAD_DIM // 2
    ri = lax.broadcasted_iota(jnp.int32, (LANES, LANES), 0) < HEAD_DIM
    ci = lax.broadcasted_iota(jnp.int32, (LANES, LANES), 1) < HEAD_DIM
    same_head = ri == ci
    scale = HEAD_DIM ** -0.5
    TAIL_B, DEC_F, DEC_B, TAIL_F = range(4)

    def rot(x, rows):
        sw = jnp.where(swap_up, pltpu.roll(x, LANES - HEAD_DIM // 2, 1),
                       pltpu.roll(x, HEAD_DIM // 2, 1))
        return x * cos_ref[rows, :] + sw * sin_ref[rows, :]

    @pl.when((p == 0) & (c == 0))
    def _init():
        hf[...] = jnp.zeros_like(hf)
        hb[...] = jnp.zeros_like(hb)
        d = (lax.broadcasted_iota(jnp.int32, (L, L), 0)
             - lax.broadcasted_iota(jnp.int32, (L, L), 1)).astype(F32)
        for h in range(2 * npair):
            lf = lgf[:, HEAD_DIM * h:HEAD_DIM * h + 1]
            lb = lgb[:, HEAD_DIM * h:HEAD_DIM * h + 1]
            dm[h] = jnp.exp(jnp.where(d >= 0, d * lf, -d * lb))
        jj = lax.broadcasted_iota(jnp.int32, (L, 1), 0).astype(F32)
        ptab[TAIL_B] = jnp.exp(jj * lgb)
        ptab[DEC_F] = jnp.exp((jj + 1.0) * lgf)
        ptab[DEC_B] = jnp.exp((L - jj) * lgb)
        ptab[TAIL_F] = jnp.exp((L - 1.0 - jj) * lgf)

    def backward_chunk(rows, ch):
        k = k_ref[rows, :].astype(F32)
        decay_b = jnp.exp(L * lgb)
        for pr in range(npair):
            sl = slice(LANES * pr, LANES * (pr + 1))
            kp = rot(k[:, sl], rows) * scale
            kt = (kp * ptab[TAIL_B, :, sl]).T.astype(BF16)
            hbs[ch, pr] = hb[pr].astype(BF16)
            upd = jnp.dot(kt, v_ref[rows, sl], preferred_element_type=F32)
            hb[pr] = jnp.where(same_head, hb[pr] * decay_b[:, sl] + upd, 0.0)

    def forward_chunk(rows, ch):
        q = q_ref[rows, :].astype(F32)
        k = k_ref[rows, :].astype(F32)
        decay_f = jnp.exp(L * lgf)
        ys = []
        for pr in range(npair):
            sl = slice(LANES * pr, LANES * (pr + 1))
            qp = rot(q[:, sl], rows)
            kp = rot(k[:, sl], rows) * scale
            kb = kp.astype(BF16)
            vp = v_ref[rows, sl]
            res = []
            for hh in range(2):
                keep = first if hh == 0 else jnp.logical_not(first)
                qm = jnp.where(keep, qp, 0.0).astype(BF16)
                s = lax.dot_general(qm, kb, (((1,), (1,)), ((), ())), preferred_element_type=F32)
                a = (s * dm[2 * pr + hh]).astype(BF16)
                res.append(jnp.dot(a, vp, preferred_element_type=F32))
            y = jnp.where(first, res[0], res[1])
            y = y + jnp.dot((qp * ptab[DEC_F, :, sl]).astype(BF16), hf[pr].astype(BF16),
                            preferred_element_type=F32)
            y = y + jnp.dot((qp * ptab[DEC_B, :, sl]).astype(BF16), hbs[ch, pr],
                            preferred_element_type=F32)
            kt = (kp * ptab[TAIL_F, :, sl]).T.astype(BF16)
            upd = jnp.dot(kt, vp, preferred_element_type=F32)
            hf[pr] = jnp.where(same_head, hf[pr] * decay_f[:, sl] + upd, 0.0)
            ys.append(y)
        y = jnp.concatenate(ys, axis=1)
        gm = gm_ref[...]
        mu = _split2_dot(y, gm)
        d = y - mu
        var = _split2_dot(d * d, gm)
        yn = d * lax.rsqrt(var + EPS) * gn_ref[...]
        g = g_ref[rows, :].astype(F32)
        o_ref[rows, :] = (g * jax.nn.sigmoid(g) * yn).astype(o_ref.dtype)

    @pl.when(p == 0)
    def _backward_states():
        blk = nc // cps - 1 - c
        for sub in reversed(range(cps)):
            backward_chunk(slice(sub * L, (sub + 1) * L), blk * cps + sub)

    @pl.when(p == 1)
    def _forward():
        for sub in range(cps):
            forward_chunk(slice(sub * L, (sub + 1) * L), c * cps + sub)


def _retention(proj, cos_t, sin_t, lgf, lgb, gn, gmat, batch, seq, width):
    L = RET_CHUNK
    nc = seq // L
    nw = width // LANES
    t = proj.shape[0]

    nb = nc // RET_CHUNKS_PER_STEP
    lb = L * RET_CHUNKS_PER_STEP

    def kv_map(col):
        return lambda b, p, c: (b * nb + _chunk_of(p, c, nb), col)

    def fwd_map(col):
        return lambda b, p, c: (b * nb + p * c, col)

    tab_map = lambda b, p, c: (_chunk_of(p, c, nb), 0)
    return pl.pallas_call(
        functools.partial(_ret_kernel, nc=nc),
        grid=(batch, 2, nb),
        in_specs=[pl.BlockSpec((lb, width), fwd_map(0)), pl.BlockSpec((lb, width), kv_map(1)),
                  pl.BlockSpec((lb, width), kv_map(2)), pl.BlockSpec((lb, width), fwd_map(3)),
                  pl.BlockSpec((lb, LANES), tab_map), pl.BlockSpec((lb, LANES), tab_map),
                  _resident((1, width)), _resident((1, width)), _resident((1, width)),
                  _resident((width, width))],
        out_specs=pl.BlockSpec((lb, width), fwd_map(0)),
        out_shape=jax.ShapeDtypeStruct((t, width), BF16),
        scratch_shapes=[pltpu.VMEM((nw, LANES, LANES), F32), pltpu.VMEM((nw, LANES, LANES), F32),
                        pltpu.VMEM((nc, nw, LANES, LANES), BF16), pltpu.VMEM((2 * nw, L, L), F32),
                        pltpu.VMEM((4, L, width), F32)],
        compiler_params=_params("arbitrary", "arbitrary", "arbitrary"),
        name="retention",
    )(proj, proj, proj, proj, cos_t, sin_t, lgf, lgb, gn, gmat)


def _na_kernel(q_ref, kp_ref, kc_ref, kn_ref, vp_ref, vc_ref, vn_ref, bias_ref, o_ref,
               kbuf, vtb, *, rows_total):
    i = pl.program_id(1)
    halo = kp_ref.shape[0]
    main = kc_ref.shape[0]
    span = main + 2 * halo
    nkeys = NA_WIN_R * GRID_W
    npair = o_ref.shape[1] // LANES
    kbuf[0:halo] = kp_ref[...]
    kbuf[halo:halo + main] = kc_ref[...]
    kbuf[halo + main:] = kn_ref[...]
    vtb[0, :, 0:halo] = vp_ref[...]
    vtb[0, :, halo:halo + main] = vc_ref[...]
    vtb[0, :, halo + main:] = vn_ref[...]
    vtb[1] = pltpu.roll(vtb[0], span - GRID_W, 1)
    first = _first_half_mask((1, LANES))
    scale = HEAD_DIM ** -0.5
    buf_row0 = i * NA_ROWS_PER_STEP - NA_WIN_R // 2

    def row_body(r, carry):
        gr = i * NA_ROWS_PER_STEP + r
        r0 = jnp.clip(gr - NA_WIN_R // 2, 0, rows_total - NA_WIN_R)
        boff = pl.multiple_of((NA_WIN_R - 1 - (gr - r0)) * GRID_W, GRID_W)
        rel = r0 - buf_row0
        odd = rel % 2
        off = pl.multiple_of(rel * GRID_W, GRID_W)
        base = pl.multiple_of((rel - odd) * GRID_W, LANES)
        qoff = pl.multiple_of(r * GRID_W, GRID_W)
        qrow = q_ref[pl.ds(qoff, GRID_W), :].astype(F32) * scale
        outs = []
        for pr in range(npair):
            sl = slice(LANES * pr, LANES * (pr + 1))
            qp = qrow[:, sl]
            qm = jnp.concatenate([jnp.where(first, qp, 0.0), jnp.where(first, 0.0, qp)],
                                 axis=0).astype(BF16)
            kwin = kbuf[pl.ds(off, nkeys), sl]
            st = lax.dot_general(kwin, qm, (((1,), (1,)), ((), ())), preferred_element_type=F32)
            st = st + bias_ref[pr, pl.ds(boff, nkeys), :]
            m = jnp.max(st, axis=0, keepdims=True)
            e = jnp.exp(st - m)
            l = jnp.sum(e, axis=0, keepdims=True)
            vtw = vtb[odd, sl, pl.ds(base, nkeys)]
            ot = jnp.dot(vtw, e.astype(BF16), preferred_element_type=F32) / l
            o = ot.T
            outs.append(jnp.where(first, o[0:GRID_W], o[GRID_W:2 * GRID_W]))
        o_ref[pl.ds(qoff, GRID_W), :] = jnp.concatenate(outs, axis=1).astype(o_ref.dtype)
        return carry

    lax.fori_loop(0, NA_ROWS_PER_STEP, row_body, 0, unroll=NA_ROW_UNROLL)


def _neighborhood_attention(proj, vt, bias, batch, seq, width, q_col, k_col):
    assert 2 * GRID_W == LANES
    rows = seq // GRID_W
    nstep = rows // NA_ROWS_PER_STEP
    main = NA_ROWS_PER_STEP * GRID_W
    halo = (NA_WIN_R // 2) * GRID_W
    per = main // halo
    nhalo = seq // halo
    t = proj.shape[0]

    main_i = lambda b, i: b * nstep + i
    prev_i = lambda b, i: b * nhalo + jnp.maximum(i * per - 1, 0)
    next_i = lambda b, i: b * nhalo + jnp.minimum((i + 1) * per, nhalo - 1)
    rowblk = lambda f, col: (lambda b, i: (f(b, i), col))
    colblk = lambda f: (lambda b, i: (0, f(b, i)))

    return pl.pallas_call(
        functools.partial(_na_kernel, rows_total=rows),
        grid=(batch, nstep),
        in_specs=[pl.BlockSpec((main, width), rowblk(main_i, q_col)),
                  pl.BlockSpec((halo, width), rowblk(prev_i, k_col)),
                  pl.BlockSpec((main, width), rowblk(main_i, k_col)),
                  pl.BlockSpec((halo, width), rowblk(next_i, k_col)),
                  pl.BlockSpec((width, halo), colblk(prev_i)),
                  pl.BlockSpec((width, main), colblk(main_i)),
                  pl.BlockSpec((width, halo), colblk(next_i)),
                  _resident(bias.shape)],
        out_specs=pl.BlockSpec((main, width), rowblk(main_i, 0)),
        out_shape=jax.ShapeDtypeStruct((t, width), BF16),
        scratch_shapes=[pltpu.VMEM((main + 2 * halo, width), BF16),
                        pltpu.VMEM((2, width, main + 2 * halo), BF16)],
        compiler_params=_params("parallel", "parallel"),
        name="neighborhood_attention",
    )(proj, proj, proj, proj, vt, vt, vt, bias)


def _na_bias_table(rpb):
    qc = jnp.arange(GRID_W)
    kc = jnp.arange(GRID_W)
    cstart = jnp.clip(qc - NA_WIN_C // 2, 0, GRID_W - NA_WIN_C)
    valid = (kc[None, :] >= cstart[:, None]) & (kc[None, :] < cstart[:, None] + NA_WIN_C)
    dc = jnp.clip(kc[None, :] - qc[:, None], -(NA_WIN_C - 1), NA_WIN_C - 1) + (NA_WIN_C - 1)
    onehot = (dc[None] == jnp.arange(2 * NA_WIN_C - 1)[:, None, None]).astype(F32)
    toep = jnp.einsum('hrd,dqk->hrqk', rpb.astype(F32), onehot, precision=lax.Precision.HIGHEST)
    toep = jnp.where(valid[None, None], toep, NEG_BIG)
    nh, ndr = rpb.shape[0], rpb.shape[1]
    tab = toep.reshape(nh // 2, 2, ndr, GRID_W, GRID_W)
    tab = jnp.transpose(tab, (0, 2, 4, 1, 3))
    return tab.reshape(nh // 2, ndr * GRID_W, 2 * GRID_W)


def _ssd_decay_kernel(dt_ref, dtb_ref, alog_ref, cs_ref, cse_ref, rf_ref, rb_ref):
    L = SSD_CHUNK
    tri = (lax.broadcasted_iota(jnp.int32, (L, L), 0)
           >= lax.broadcasted_iota(jnp.int32, (L, L), 1)).astype(BF16)
    raw = dt_ref[...] + dtb_ref[...]
    dt = jnp.maximum(raw, 0.0) + jnp.log1p(jnp.exp(-jnp.abs(raw)))
    a = dt * (-jnp.exp(alog_ref[...])) * LOG2_E
    ldt = jnp.log2(dt)
    for k in range(dt_ref.shape[0] // L):
        rows = slice(k * L, (k + 1) * L)
        cs = _split3_dot(tri, a[rows])
        cse = cs - a[rows]
        cs_ref[rows, :] = cs
        cse_ref[rows, :] = cse
        rf_ref[:, rows] = (cs - ldt[rows]).T
        rb_ref[:, rows] = (cse + ldt[rows]).T


def _ssd_decay(dt, dtb, alog, chunks_per_step=8):
    L = SSD_CHUNK * chunks_per_step
    t = dt.shape[0]
    col = pl.BlockSpec((L, LANES), lambda i: (i, 0))
    rowm = pl.BlockSpec((LANES, L), lambda i: (0, i))
    return pl.pallas_call(
        _ssd_decay_kernel,
        grid=(t // L,),
        in_specs=[col, _resident((1, LANES)), _resident((1, LANES))],
        out_specs=[col, col, rowm, rowm],
        out_shape=[jax.ShapeDtypeStruct((t, LANES), F32), jax.ShapeDtypeStruct((t, LANES), F32),
                   jax.ShapeDtypeStruct((LANES, t), F32), jax.ShapeDtypeStruct((LANES, t), F32)],
        compiler_params=_params("parallel"),
        name="ssd_decay",
    )(dt, dtb, alog)


def _ssd_kernel(x_ref, b_ref, c_ref, z_ref, cs_ref, cse_ref, rf_ref, rb_ref, dsk_ref, ng_ref,
                o_ref, hf, hb, hbs, *, nc, hpg, groups):
    p = pl.program_id(1)
    c = pl.program_id(2)
    L = SSD_CHUNK
    cps = x_ref.shape[0] // L
    assert L == SSD_STATE == LANES
    gw = hpg * SSD_HEADDIM
    npair = gw // LANES
    first = _first_half_mask((1, LANES))
    rj = lax.broadcasted_iota(jnp.int32, (L, L), 0)
    rl = lax.broadcasted_iota(jnp.int32, (L, L), 1)
    lower = rj >= rl

    @pl.when((p == 0) & (c == 0))
    def _init():
        hf[...] = jnp.zeros_like(hf)
        hb[...] = jnp.zeros_like(hb)

    def pair_scale(vals):
        return jnp.where(first, vals[0], vals[1])

    def row(m, lane):
        return m[lane:lane + 1, :]

    def col_bcast(m, lane):
        return jnp.broadcast_to(m[:, lane:lane + 1], (L, LANES))

    def backward_chunk(rows, ch):
        cs = cs_ref[rows, :]
        rows_b = rb_ref[:, rows]
        tot = cs[L - 1:L, :]
        for g in range(groups):
            bt = b_ref[rows, SSD_STATE * g:SSD_STATE * (g + 1)].astype(F32).T
            for pr in range(npair):
                idx = g * npair + pr
                xp = x_ref[rows, LANES * idx:LANES * (idx + 1)]
                upd, dec = [], []
                for hh in range(2):
                    lb = 2 * hpg * g + hpg + 2 * pr + hh
                    srow = jnp.exp2(row(rows_b, lb))
                    upd.append(jnp.dot((bt * srow).astype(BF16), xp, preferred_element_type=F32))
                    dec.append(jnp.exp2(tot[:, lb:lb + 1]))
                hbs[ch, idx] = hb[idx].astype(BF16)
                hb[idx] = hb[idx] * pair_scale(dec) + pair_scale(upd)

    def forward_chunk(rows, ch):
        cs = cs_ref[rows, :]
        cse = cse_ref[rows, :]
        rows_f = rf_ref[:, rows]
        rows_b = rb_ref[:, rows]
        tot = cs[L - 1:L, :]
        for g in range(groups):
            bm = b_ref[rows, SSD_STATE * g:SSD_STATE * (g + 1)]
            cm = c_ref[rows, SSD_STATE * g:SSD_STATE * (g + 1)]
            bt = bm.astype(F32).T
            cb = lax.dot_general(cm, bm, (((1,), (1,)), ((), ())), preferred_element_type=F32)
            cf32 = cm.astype(F32)
            ys = []
            for pr in range(npair):
                idx = g * npair + pr
                sl = slice(LANES * idx, LANES * (idx + 1))
                xp = x_ref[rows, sl]
                hf_b = hf[idx].astype(BF16)
                hb_b = hbs[ch, idx]
                res, upd, dec = [], [], []
                for hh in range(2):
                    lf = 2 * hpg * g + 2 * pr + hh
                    lb = lf + hpg
                    cfc = col_bcast(cs, lf)
                    cbc = col_bcast(cse, lb)
                    e = jnp.where(lower, cfc - row(rows_f, lf), row(rows_b, lb) - cbc)
                    m = (cb * jnp.exp2(e)).astype(BF16)
                    c_f = (cf32 * jnp.exp2(cfc)).astype(BF16)
                    c_b = (cf32 * jnp.exp2(tot[:, lb:lb + 1] - cbc)).astype(BF16)
                    res.append(jnp.dot(m, xp, preferred_element_type=F32)
                               + jnp.dot(c_f, hf_b, preferred_element_type=F32)
                               + jnp.dot(c_b, hb_b, preferred_element_type=F32))
                    srow = jnp.exp2(tot[:, lf:lf + 1] - row(rows_f, lf))
                    upd.append(jnp.dot((bt * srow).astype(BF16), xp, preferred_element_type=F32))
                    dec.append(jnp.exp2(tot[:, lf:lf + 1]))
                hf[idx] = hf[idx] * pair_scale(dec) + pair_scale(upd)
                y = pair_scale(res) + xp.astype(F32) * dsk_ref[:, sl]
                z = z_ref[rows, sl].astype(F32)
                ys.append(y * (z * jax.nn.sigmoid(z)))
            y = jnp.concatenate(ys, axis=1)
            gsl = slice(gw * g, gw * (g + 1))
            o_ref[rows, gsl] = _rms_rows(y, ng_ref[:, gsl]).astype(o_ref.dtype)

    @pl.when(p == 0)
    def _backward_states():
        blk = nc // cps - 1 - c
        for sub in reversed(range(cps)):
            backward_chunk(slice(sub * L, (sub + 1) * L), blk * cps + sub)

    @pl.when(p == 1)
    def _forward():
        for sub in range(cps):
            forward_chunk(slice(sub * L, (sub + 1) * L), c * cps + sub)


def _ssd_scan(xbc, z, decay, dsk, ng, batch, seq, inner, hpg):
    cs, cse, rows_f, rows_b = decay
    L = SSD_CHUNK
    nc = seq // L
    groups = inner // (hpg * SSD_HEADDIM)
    npairs = inner // LANES
    bc_w = groups * SSD_STATE
    t = xbc.shape[0]
    b_col = inner // bc_w

    nb = nc // SSD_CHUNKS_PER_STEP
    lb = L * SSD_CHUNKS_PER_STEP

    def any_map(col):
        return lambda b, p, c: (b * nb + _chunk_of(p, c, nb), col)

    def fwd_map(col):
        return lambda b, p, c: (b * nb + p * c, col)

    any_t = lambda b, p, c: (0, b * nb + _chunk_of(p, c, nb))
    fwd_t = lambda b, p, c: (0, b * nb + p * c)
    return pl.pallas_call(
        functools.partial(_ssd_kernel, nc=nc, hpg=hpg, groups=groups),
        grid=(batch, 2, nb),
        in_specs=[pl.BlockSpec((lb, inner), any_map(0)),
                  pl.BlockSpec((lb, bc_w), any_map(b_col)),
                  pl.BlockSpec((lb, bc_w), fwd_map(b_col + 1)),
                  pl.BlockSpec((lb, inner), fwd_map(0)),
                  pl.BlockSpec((lb, LANES), any_map(0)), pl.BlockSpec((lb, LANES), fwd_map(0)),
                  pl.BlockSpec((LANES, lb), fwd_t), pl.BlockSpec((LANES, lb), any_t),
                  _resident((1, inner)), _resident((1, inner))],
        out_specs=pl.BlockSpec((lb, inner), fwd_map(0)),
        out_shape=jax.ShapeDtypeStruct((t, inner), BF16),
        scratch_shapes=[pltpu.VMEM((npairs, SSD_STATE, LANES), F32),
                        pltpu.VMEM((npairs, SSD_STATE, LANES), F32),
                        pltpu.VMEM((nc, npairs, SSD_STATE, LANES), BF16)],
        compiler_params=_params("arbitrary", "arbitrary", "arbitrary"),
        name="ssd_scan",
    )(xbc, xbc, xbc, z, cs, cse, rows_f, rows_b, dsk, ng)


def _rotary_tables(seq):
    half = HEAD_DIM // 2
    inv = 1.0 / (ROPE_BASE ** (jnp.arange(half, dtype=F32) / half))
    ang = jnp.arange(seq).astype(F32)[:, None] * inv[None, :]
    cos = jnp.cos(ang)
    sin = jnp.sin(ang)
    reps = LANES // HEAD_DIM
    cos_t = jnp.tile(jnp.concatenate([cos, cos], axis=1), (1, reps))
    sin_t = jnp.tile(jnp.concatenate([-sin, sin], axis=1), (1, reps))
    return cos_t, sin_t


def _group_dt_lanes(v, hpg):
    lead = v.shape[:-2]
    heads = v.shape[-1]
    groups = heads // hpg
    assert 2 * heads <= LANES
    vg = v.reshape(*lead, 2, groups, hpg)
    vg = jnp.moveaxis(vg, -3, -2).reshape(*lead, 2 * heads)
    pad = [(0, 0)] * (vg.ndim - 1) + [(0, LANES - 2 * heads)]
    return jnp.pad(vg, pad)


def kernel(x, norm_mix_pre, norm_mix_post, norm_ffn_pre, norm_ffn_post, ab_w_in, ab_ret_decay_logit, ab_ret_gn_g, ab_na_rpb, ab_w_out, c_w_in, c_conv_w, c_conv_b, c_dt_bias, c_a_log, c_d_skip, c_norm_g, c_w_out, ffn_w_up, ffn_conv_w, ffn_conv_b, ffn_w_down):
    batch, seq, d = x.shape
    depth = norm_mix_pre.shape[0]
    t = batch * seq
    ret_w = RET_HEADS * HEAD_DIM
    na_w = NA_HEADS * HEAD_DIM
    inner = c_w_out.shape[1]
    heads = inner // SSD_HEADDIM
    hpg = heads // SSD_GROUPS
    xbc_w = inner + 2 * SSD_GROUPS * SSD_STATE
    assert ret_w == na_w and ab_w_in.shape[2] == 4 * ret_w + 3 * na_w

    h = x.reshape(t, d).astype(F32)
    cos_t, sin_t = _rotary_tables(seq)
    gmat = jnp.kron(jnp.eye(RET_HEADS, dtype=F32),
                    jnp.full((HEAD_DIM, HEAD_DIM), 1.0 / HEAD_DIM, F32)).astype(BF16)
    row = lambda v: v.reshape(1, -1).astype(F32)

    for layer in range(depth):
        i = layer // 2
        if layer % 2 == 0:
            n_tok = 4 * ret_w + 2 * na_w
            proj, na_vt = _norm_proj(h, row(norm_mix_pre[layer]),
                                     ab_w_in[i][:, :n_tok].astype(BF16),
                                     ab_w_in[i][:, n_tok:].T.astype(BF16))
            lg = -jax.nn.softplus(-ab_ret_decay_logit[i].astype(F32))
            lgf = jnp.repeat(lg[0], HEAD_DIM).reshape(1, ret_w)
            lgb = jnp.repeat(lg[1], HEAD_DIM).reshape(1, ret_w)
            ret = _retention(proj, cos_t, sin_t, lgf, lgb, row(ab_ret_gn_g[i]), gmat,
                             batch, seq, ret_w)
            na = _neighborhood_attention(proj, na_vt, _na_bias_table(ab_na_rpb[i]), batch, seq, na_w,
                                         4 * ret_w // na_w, 4 * ret_w // na_w + 1)
            h = _out_proj([ret, na], ab_w_out[i].astype(BF16), row(norm_mix_post[layer]), h)
        else:
            w_in = c_w_in[i]
            wz = w_in[:, :inner].astype(BF16)
            wx = w_in[:, inner:inner + xbc_w].astype(BF16)
            wdt = _group_dt_lanes(w_in[:, inner + xbc_w:].reshape(d, 2, heads), hpg).astype(BF16)
            z, xbc, dtr = _ssd_inproj(h, row(norm_mix_pre[layer]), wz, wx, wdt,
                                      c_conv_w[i].astype(F32), row(c_conv_b[i]), seq)
            dtb = _group_dt_lanes(c_dt_bias[i].astype(F32), hpg).reshape(1, -1)
            alog = _group_dt_lanes(c_a_log[i].astype(F32), hpg).reshape(1, -1)
            dsk = jnp.repeat(c_d_skip[i].astype(F32), SSD_HEADDIM).reshape(1, inner)
            y = _ssd_scan(xbc, z, _ssd_decay(dtr, dtb, alog), dsk, row(c_norm_g[i]),
                          batch, seq, inner, hpg)
            h = _out_proj([y], c_w_out[i].astype(BF16), row(norm_mix_post[layer]), h)
        h = _ffn(h, row(norm_ffn_pre[layer]), ffn_w_up[layer].astype(BF16),
                 ffn_conv_w[layer].astype(F32), row(ffn_conv_b[layer]),
                 ffn_w_down[layer].astype(BF16), row(norm_ffn_post[layer]), seq)
    return h.reshape(batch, seq, d).astype(x.dtype)
```
